```python
import math
import jax, jax.numpy as jnp
from jax import lax
import numpy as np


D_MODEL = 4096
BATCH = 8
SEQ = 2048
DEPTH = 2
DEC_BATCH = 4
DEC_SEQ = 4096
PAST_LEN = 128

N_EVEN = (DEPTH + 1) // 2
N_ODD = DEPTH // 2
EPS = 1e-6

D_CONV = D_MODEL // 2
CONV_A_WIDTH = 3
D_RNN = D_MODEL
RNN_HEADS = 16
RNN_HEAD_DIM = D_RNN // RNN_HEADS
CONV_B_WIDTH = 4
LRU_C = 8.0
D_IN_AB = 3 * D_CONV + 2 * D_RNN
HEAD_DIM = 128
N_Q_HEADS = D_MODEL // HEAD_DIM
N_KV_HEADS = N_Q_HEADS // 4
GROUP = N_Q_HEADS // N_KV_HEADS
WINDOW = 128
BLOCK = 128
D_QKV = (N_Q_HEADS + 2 * N_KV_HEADS) * HEAD_DIM
NEG_INF = -1e30
N_EXPERTS = 16
D_EXPERT = D_MODEL // 2
CAPACITY_FACTOR = 2

kernel_name = 'hybrid_bidir_conv_rglru_swa_ecmoe'


def rmsnorm(x, g):
    xf = x.astype(jnp.float32)
    y = xf * lax.rsqrt(jnp.mean(xf * xf, axis=-1, keepdims=True) + EPS) * g.astype(jnp.float32)
    return y.astype(x.dtype)


def depthwise_conv(x, w, pad):
    c = x.shape[-1]
    return lax.conv_general_dilated(
        x, w[:, None, :].astype(x.dtype), window_strides=(1,), padding=[pad],
        dimension_numbers=('NWC', 'WIO', 'NWC'), feature_group_count=c)


def rglru_bidirectional(x, gate_a_w, gate_a_b, gate_x_w, gate_x_b, lru_lambda):
    b, t, _ = x.shape
    xf = x.astype(jnp.float32)
    xh = xf.reshape(b, t, RNN_HEADS, RNN_HEAD_DIM)
    r = jax.nn.sigmoid(jnp.einsum('bthi,nhij->nbthj', xh, gate_a_w.astype(jnp.float32)).reshape(2, b, t, D_RNN)
                       + gate_a_b.astype(jnp.float32)[:, None, None, :])
    i = jax.nn.sigmoid(jnp.einsum('bthi,nhij->nbthj', xh, gate_x_w.astype(jnp.float32)).reshape(2, b, t, D_RNN)
                       + gate_x_b.astype(jnp.float32)[:, None, None, :])
    log_a = -LRU_C * r * jax.nn.softplus(-lru_lambda.astype(jnp.float32))[:, None, None, :]
    a = jnp.exp(log_a)
    u = jnp.sqrt(-jnp.expm1(2.0 * log_a)) * (i * xf[None])
    a = jnp.stack([a[0], jnp.flip(a[1], axis=1)])
    u = jnp.stack([u[0], jnp.flip(u[1], axis=1)])
    a_tm = jnp.moveaxis(a, 2, 0)
    u_tm = jnp.moveaxis(u, 2, 0)

    def step(h, inp):
        a_t, u_t = inp
        h = a_t * h + u_t
        return h, h

    _, hs = lax.scan(step, jnp.zeros((2, b, D_RNN), jnp.float32), (a_tm, u_tm))
    hs = jnp.moveaxis(hs, 0, 2)
    y = hs[0] + jnp.flip(hs[1], axis=1)
    return y.astype(x.dtype)


def conv_rglru_mixer(h, w_in, conv_a, conv_b, conv_b_bias, gate_a_w, gate_a_b,
                     gate_x_w, gate_x_b, lru_lambda, w_out):
    proj = jnp.einsum('btd,de->bte', h, w_in.astype(h.dtype))
    b_gate, c_gate, v, g_rnn, x_rnn = jnp.split(
        proj, [D_CONV, 2 * D_CONV, 3 * D_CONV, 3 * D_CONV + D_RNN], axis=-1)
    y_a = b_gate * depthwise_conv(c_gate * v, conv_a, (1, 1))
    xc = depthwise_conv(x_rnn, conv_b, (2, 1)) + conv_b_bias.astype(h.dtype)
    y_b = jax.nn.gelu(g_rnn) * rglru_bidirectional(xc, gate_a_w, gate_a_b, gate_x_w, gate_x_b, lru_lambda)
    y = jnp.concatenate([y_a, y_b], axis=-1)
    return jnp.einsum('bte,ed->btd', y, w_out.astype(h.dtype)).astype(h.dtype)


def alibi_slopes():
    return jnp.exp2(-8.0 * (jnp.arange(N_Q_HEADS, dtype=jnp.float32) + 1.0) / N_Q_HEADS)


def window_attention(h, w_qkv, w_o, sink):
    b, t, _ = h.shape
    nb = t // BLOCK
    qkv = jnp.einsum('btd,de->bte', h, w_qkv.astype(h.dtype))
    q, k, v = jnp.split(qkv, [N_Q_HEADS * HEAD_DIM, (N_Q_HEADS + N_KV_HEADS) * HEAD_DIM], axis=-1)
    qb = q.reshape(b, nb, BLOCK, N_KV_HEADS, GROUP, HEAD_DIM)
    pad = ((0, 0), (BLOCK, BLOCK), (0, 0))
    kp = jnp.pad(k, pad).reshape(b, nb + 2, BLOCK, N_KV_HEADS, HEAD_DIM)
    vp = jnp.pad(v, pad).reshape(b, nb + 2, BLOCK, N_KV_HEADS, HEAD_DIM)
    kw = jnp.concatenate([kp[:, :-2], kp[:, 1:-1], kp[:, 2:]], axis=2)
    vw = jnp.concatenate([vp[:, :-2], vp[:, 1:-1], vp[:, 2:]], axis=2)
    scores = jnp.einsum('bnqhgd,bnkhd->bnhgqk', qb, kw,
                        preferred_element_type=jnp.float32) * (HEAD_DIM ** -0.5)
    blk = jnp.arange(nb)[:, None]
    q_pos = blk * BLOCK + jnp.arange(BLOCK)[None, :]
    k_pos = (blk - 1) * BLOCK + jnp.arange(3 * BLOCK)[None, :]
    dist = jnp.abs(q_pos[:, :, None] - k_pos[:, None, :])
    valid = (dist <= WINDOW) & ((k_pos >= 0) & (k_pos < t))[:, None, :]
    slopes = alibi_slopes().reshape(N_KV_HEADS, GROUP)
    bias = -slopes[None, :, :, None, None] * dist[:, None, None].astype(jnp.float32)
    scores = jnp.where(valid[None, :, None, None], scores + bias[None], NEG_INF)
    sink_b = sink.astype(jnp.float32).reshape(N_KV_HEADS, GROUP)[None, None, :, :, None, None]
    m = jnp.maximum(jnp.max(scores, axis=-1, keepdims=True), sink_b)
    p = jnp.exp(scores - m)
    probs = p / (jnp.sum(p, axis=-1, keepdims=True) + jnp.exp(sink_b - m))
    out = jnp.einsum('bnhgqk,bnkhd->bnqhgd', probs.astype(v.dtype), vw)
    out = out.reshape(b, t, N_Q_HEADS * HEAD_DIM)
    return jnp.einsum('bte,ed->btd', out, w_o.astype(h.dtype)).astype(h.dtype)


def expert_choice_moe(h, w_router, w_gate, w_up, w_down):
    b, t, d = h.shape
    n = b * t
    cap = max(1, CAPACITY_FACTOR * n // N_EXPERTS)
    xf = h.reshape(n, d)
    logits = jnp.einsum('nd,de->ne', xf, w_router.astype(h.dtype)).astype(jnp.float32)
    affinity = jax.nn.softmax(logits, axis=-1)
    gates, idx = lax.top_k(affinity.T, cap)
    xe = xf[idx]
    hg = jnp.einsum('ecd,edf->ecf', xe, w_gate.astype(h.dtype))
    hu = jnp.einsum('ecd,edf->ecf', xe, w_up.astype(h.dtype))
    ye = jnp.einsum('ecf,efd->ecd', jax.nn.silu(hg) * hu, w_down.astype(h.dtype))
    ye = (ye * gates[..., None].astype(ye.dtype)).astype(h.dtype)
    out = jnp.zeros((n, d), h.dtype).at[idx.reshape(-1)].add(ye.reshape(-1, d))
    return out.reshape(b, t, d)


def trunk(x, mix_norm, ffn_norm, final_norm, ab_w_in, ab_conv_a, ab_conv_b, ab_conv_b_bias,
          ab_gate_a_w, ab_gate_a_b, ab_gate_x_w, ab_gate_x_b, ab_lambda, ab_w_out,
          attn_w_qkv, attn_w_o, attn_sink, moe_w_router, moe_w_gate, moe_w_up, moe_w_down):
    for layer in range(DEPTH):
        j = layer // 2
        hn = rmsnorm(x, mix_norm[layer])
        if layer % 2 == 0:
            x = x + conv_rglru_mixer(hn, ab_w_in[j], ab_conv_a[j], ab_conv_b[j], ab_conv_b_bias[j],
                                     ab_gate_a_w[j], ab_gate_a_b[j], ab_gate_x_w[j], ab_gate_x_b[j],
                                     ab_lambda[j], ab_w_out[j])
        else:
            x = x + window_attention(hn, attn_w_qkv[j], attn_w_o[j], attn_sink[j])
        x = x + expert_choice_moe(rmsnorm(x, ffn_norm[layer]), moe_w_router[layer],
                                  moe_w_gate[layer], moe_w_up[layer], moe_w_down[layer])
    return rmsnorm(x, final_norm)


def setup_inputs(seed: int = 0) -> dict:
    key = jax.random.key(seed)
    ks = jax.random.split(key, 24)
    f32 = jnp.float32

    def nrm(k, shape, scale):
        return jax.random.normal(k, shape, f32) * scale

    a0 = jax.random.uniform(ks[13], (N_EVEN, 2, D_RNN), f32, 0.9, 0.999)
    return {
        'x_prompt': nrm(ks[0], (BATCH, SEQ, D_MODEL), 1.0),
        'x_sample': nrm(ks[1], (DEC_BATCH, DEC_SEQ, D_MODEL), 1.0),
        'mix_norm': 1.0 + nrm(ks[2], (DEPTH, D_MODEL), 0.02),
        'ffn_norm': 1.0 + nrm(ks[3], (DEPTH, D_MODEL), 0.02),
        'final_norm': 1.0 + nrm(ks[4], (D_MODEL,), 0.02),
        'ab_w_in': nrm(ks[5], (N_EVEN, D_MODEL, D_IN_AB), D_MODEL ** -0.5),
        'ab_conv_a': nrm(ks[6], (N_EVEN, CONV_A_WIDTH, D_CONV), CONV_A_WIDTH ** -0.5),
        'ab_conv_b': nrm(ks[7], (N_EVEN, CONV_B_WIDTH, D_RNN), CONV_B_WIDTH ** -0.5),
        'ab_conv_b_bias': nrm(ks[8], (N_EVEN, D_RNN), 0.01),
        'ab_gate_a_w': nrm(ks[9], (N_EVEN, 2, RNN_HEADS, RNN_HEAD_DIM, RNN_HEAD_DIM), RNN_HEAD_DIM ** -0.5),
        'ab_gate_a_b': nrm(ks[10], (N_EVEN, 2, D_RNN), 0.01),
        'ab_gate_x_w': nrm(ks[11], (N_EVEN, 2, RNN_HEADS, RNN_HEAD_DIM, RNN_HEAD_DIM), RNN_HEAD_DIM ** -0.5),
        'ab_gate_x_b': nrm(ks[12], (N_EVEN, 2, D_RNN), 0.01),
        'ab_lambda': jnp.log(a0) - jnp.log1p(-a0),
        'ab_w_out': nrm(ks[14], (N_EVEN, D_CONV + D_RNN, D_MODEL), (D_CONV + D_RNN) ** -0.5),
        'attn_w_qkv': nrm(ks[15], (N_ODD, D_MODEL, D_QKV), D_MODEL ** -0.5),
        'attn_w_o': nrm(ks[16], (N_ODD, N_Q_HEADS * HEAD_DIM, D_MODEL), (N_Q_HEADS * HEAD_DIM) ** -0.5),
        'attn_sink': nrm(ks[17], (N_ODD, N_Q_HEADS), 0.5),
        'moe_w_router': nrm(ks[18], (DEPTH, D_MODEL, N_EXPERTS), D_MODEL ** -0.5),
        'moe_w_gate': nrm(ks[19], (DEPTH, N_EXPERTS, D_MODEL, D_EXPERT), D_MODEL ** -0.5),
        'moe_w_up': nrm(ks[20], (DEPTH, N_EXPERTS, D_MODEL, D_EXPERT), D_MODEL ** -0.5),
        'moe_w_down': nrm(ks[21], (DEPTH, N_EXPERTS, D_EXPERT, D_MODEL), D_EXPERT ** -0.5),
    }


def reference(x_prompt, x_sample, mix_norm, ffn_norm, final_norm, ab_w_in, ab_conv_a, ab_conv_b,
              ab_conv_b_bias, ab_gate_a_w, ab_gate_a_b, ab_gate_x_w, ab_gate_x_b, ab_lambda, ab_w_out,
              attn_w_qkv, attn_w_o, attn_sink, moe_w_router, moe_w_gate, moe_w_up, moe_w_down):
    y_prompt = trunk(x_prompt, mix_norm, ffn_norm, final_norm, ab_w_in, ab_conv_a, ab_conv_b,
                     ab_conv_b_bias, ab_gate_a_w, ab_gate_a_b, ab_gate_x_w, ab_gate_x_b, ab_lambda,
                     ab_w_out, attn_w_qkv, attn_w_o, attn_sink, moe_w_router, moe_w_gate, moe_w_up,
                     moe_w_down)
    y_sample = trunk(x_sample, mix_norm, ffn_norm, final_norm, ab_w_in, ab_conv_a, ab_conv_b,
                     ab_conv_b_bias, ab_gate_a_w, ab_gate_a_b, ab_gate_x_w, ab_gate_x_b, ab_lambda,
                     ab_w_out, attn_w_qkv, attn_w_o, attn_sink, moe_w_router, moe_w_gate, moe_w_up,
                     moe_w_down)
    return (y_prompt, y_sample)
```

```python
import functools
import math

import jax
import jax.numpy as jnp
from jax import lax
from jax.experimental import pallas as pl
from jax.experimental.pallas import tpu as pltpu

EPS = 1e-6
LRU_C = 8.0
WINDOW = 128
BLOCK = 128
NEG_INF = -1e30
CAPACITY_FACTOR = 2

LANES = 128
SUBLANES = 8
VMEM_LIMIT_BYTES = 56 * 1024 * 1024

F32 = jnp.float32
BF16 = jnp.bfloat16


def _pick(n, target, quantum):
    if n <= target:
        return n
    t = (target // quantum) * quantum
    while t >= quantum:
        if n % t == 0:
            return t
        t -= quantum
    return n


def _params(sem, vmem=VMEM_LIMIT_BYTES):
    return pltpu.CompilerParams(dimension_semantics=sem, vmem_limit_bytes=vmem)


class _Seqs:
    def __init__(self, n0, t0, n1, t1, chunk):
        assert t0 % chunk == 0 and t1 % chunk == 0
        self.chunk = chunk
        self.cps0 = t0 // chunk
        self.cps1 = t1 // chunk
        self.nc0 = n0 * self.cps0
        self.nc = self.nc0 + n1 * self.cps1

    def is_first(self, c):
        return jnp.where(c < self.nc0, c % self.cps0 == 0, (c - self.nc0) % self.cps1 == 0)

    def is_last(self, c):
        return jnp.where(c < self.nc0, c % self.cps0 == self.cps0 - 1,
                         (c - self.nc0) % self.cps1 == self.cps1 - 1)


def _rmsnorm_kernel(x_ref, g_ref, o_ref):
    x = x_ref[...]
    ms = jnp.mean(x * x, axis=-1, keepdims=True)
    o_ref[...] = (x * lax.rsqrt(ms + EPS) * g_ref[...]).astype(o_ref.dtype)


def rmsnorm(x, g, out_dtype):
    m, d = x.shape
    tm = _pick(m, 512, SUBLANES)
    return pl.pallas_call(
        _rmsnorm_kernel,
        out_shape=jax.ShapeDtypeStruct((m, d), out_dtype),
        grid=(m // tm,),
        in_specs=[pl.BlockSpec((tm, d), lambda i: (i, 0)),
                  pl.BlockSpec((1, d), lambda i: (0, 0))],
        out_specs=pl.BlockSpec((tm, d), lambda i: (i, 0)),
        compiler_params=_params(("parallel",)),
        name="rmsnorm",
    )(x, g.reshape(1, d))


def _matmul_kernel(*refs, n_pairs, has_res):
    xs = refs[:n_pairs]
    ws = refs[n_pairs:2 * n_pairs]
    o_ref = refs[-1]
    acc = jnp.dot(xs[0][...], ws[0][...], preferred_element_type=F32)
    for x_ref, w_ref in zip(xs[1:], ws[1:]):
        acc = acc + jnp.dot(x_ref[...], w_ref[...], preferred_element_type=F32)
    if has_res:
        acc = acc + refs[2 * n_pairs][...]
    o_ref[...] = acc.astype(o_ref.dtype)


def matmul(xs, ws, res=None, out_dtype=F32, bm=1024, bn=1024):
    m = xs[0].shape[0]
    n = ws[0].shape[1]
    bm = _pick(m, bm, SUBLANES)
    bn = _pick(n, bn, LANES)
    in_specs = [pl.BlockSpec((bm, x.shape[1]), lambda i, j: (i, 0)) for x in xs]
    in_specs += [pl.BlockSpec((w.shape[0], bn), lambda i, j: (0, j)) for w in ws]
    args = list(xs) + list(ws)
    if res is not None:
        in_specs.append(pl.BlockSpec((bm, bn), lambda i, j: (i, j)))
        args.append(res)
    return pl.pallas_call(
        functools.partial(_matmul_kernel, n_pairs=len(xs), has_res=res is not None),
        out_shape=jax.ShapeDtypeStruct((m, n), out_dtype),
        grid=(m // bm, n // bn),
        in_specs=in_specs,
        out_specs=pl.BlockSpec((bm, bn), lambda i, j: (i, j)),
        compiler_params=_params(("parallel", "parallel")),
        name="matmul",
    )(*args)


def _halo_specs(seqs, cb, col0):
    tc = seqs.chunk
    r8 = tc // SUBLANES
    last8 = seqs.nc * r8 - 1
    main = pl.BlockSpec((tc, cb), lambda c, j: (c, col0 + j))
    prev = pl.BlockSpec((SUBLANES, cb), lambda c, j: (jnp.maximum(c * r8 - 1, 0), col0 + j))
    nxt = pl.BlockSpec((SUBLANES, cb), lambda c, j: (jnp.minimum((c + 1) * r8, last8), col0 + j))
    return [prev, main, nxt]


def _extended(prev, main, nxt, first, last):
    prev = jnp.where(first, 0.0, prev)
    nxt = jnp.where(last, 0.0, nxt)
    return jnp.concatenate([prev, main, nxt], axis=0)


def _shifted(ext, offset, tc):
    n = ext.shape[0]
    if offset == 0:
        return ext[SUBLANES:SUBLANES + tc]
    return pltpu.roll(ext, (-offset) % n, axis=0)[SUBLANES:SUBLANES + tc]


def _conv_a_kernel(cp_ref, c_ref, cn_ref, vp_ref, v_ref, vn_ref, b_ref, w_ref, o_ref, *, seqs):
    c = pl.program_id(0)
    first, last = seqs.is_first(c), seqs.is_last(c)
    tc = seqs.chunk
    z = _extended(cp_ref[...] * vp_ref[...], c_ref[...] * v_ref[...], cn_ref[...] * vn_ref[...],
                  first, last)
    w = w_ref[...]
    y = (w[0:1] * _shifted(z, -1, tc) + w[1:2] * _shifted(z, 0, tc) + w[2:3] * _shifted(z, 1, tc))
    o_ref[...] = (b_ref[...] * y).astype(o_ref.dtype)


def conv_a(proj, conv_w, seqs, d_conv):
    m = proj.shape[0]
    tc = seqs.chunk
    cb = _pick(d_conv, 512, LANES)
    nb = d_conv // cb
    specs = _halo_specs(seqs, cb, nb) + _halo_specs(seqs, cb, 2 * nb)
    specs.append(pl.BlockSpec((tc, cb), lambda c, j: (c, j)))
    specs.append(pl.BlockSpec((conv_w.shape[0], cb), lambda c, j: (0, j)))
    return pl.pallas_call(
        functools.partial(_conv_a_kernel, seqs=seqs),
        out_shape=jax.ShapeDtypeStruct((m, d_conv), BF16),
        grid=(seqs.nc, nb),
        in_specs=specs,
        out_specs=pl.BlockSpec((tc, cb), lambda c, j: (c, j)),
        compiler_params=_params(("parallel", "parallel")),
        name="conv_a",
    )(proj, proj, proj, proj, proj, proj, proj, conv_w)


def _conv_b_kernel(xp_ref, x_ref, xn_ref, w_ref, b_ref, o_ref, *, seqs):
    c = pl.program_id(0)
    tc = seqs.chunk
    x = _extended(xp_ref[...], x_ref[...], xn_ref[...], seqs.is_first(c), seqs.is_last(c))
    w = w_ref[...]
    y = (w[0:1] * _shifted(x, -2, tc) + w[1:2] * _shifted(x, -1, tc)
         + w[2:3] * _shifted(x, 0, tc) + w[3:4] * _shifted(x, 1, tc))
    o_ref[...] = y + b_ref[...]


def conv_b(proj, conv_w, bias, seqs, col0, d_rnn):
    m = proj.shape[0]
    tc = seqs.chunk
    cb = _pick(math.gcd(col0, d_rnn), 512, LANES)
    nb = d_rnn // cb
    specs = _halo_specs(seqs, cb, col0 // cb)
    specs.append(pl.BlockSpec((conv_w.shape[0], cb), lambda c, j: (0, j)))
    specs.append(pl.BlockSpec((1, cb), lambda c, j: (0, j)))
    return pl.pallas_call(
        functools.partial(_conv_b_kernel, seqs=seqs),
        out_shape=jax.ShapeDtypeStruct((m, d_rnn), F32),
        grid=(seqs.nc, nb),
        in_specs=specs,
        out_specs=pl.BlockSpec((tc, cb), lambda c, j: (c, j)),
        compiler_params=_params(("parallel", "parallel")),
        name="conv_b",
    )(proj, proj, proj, conv_w, bias.reshape(1, d_rnn))


def _gates_kernel(x_ref, w_ref, ba_ref, bx_ref, lam_ref, a_ref, u_ref, *, hd):
    x = x_ref[...]
    z = jnp.dot(x.astype(BF16), w_ref[0], preferred_element_type=F32)
    for d in range(2):
        r = jax.nn.sigmoid(z[:, (2 * d) * hd:(2 * d + 1) * hd] + ba_ref[d:d + 1, :])
        i = jax.nn.sigmoid(z[:, (2 * d + 1) * hd:(2 * d + 2) * hd] + bx_ref[d:d + 1, :])
        log_a = -LRU_C * r * jax.nn.softplus(-lam_ref[d:d + 1, :])
        a = jnp.exp(log_a)
        a_ref[d] = a
        u_ref[d] = jnp.sqrt(-jnp.tanh(log_a) * (1.0 + a * a)) * (i * x)


def lru_gates(xc, w_cat, gate_a_b, gate_x_b, lam):
    m, d_rnn = xc.shape
    heads, hd, _ = w_cat.shape
    tm = _pick(m, 1024, SUBLANES)
    out = jax.ShapeDtypeStruct((2, m, d_rnn), F32)
    vec = pl.BlockSpec((2, hd), lambda i, h: (0, h))
    blk = pl.BlockSpec((2, tm, hd), lambda i, h: (0, i, h))
    return pl.pallas_call(
        functools.partial(_gates_kernel, hd=hd),
        out_shape=(out, out),
        grid=(m // tm, heads),
        in_specs=[pl.BlockSpec((tm, hd), lambda i, h: (i, h)),
                  pl.BlockSpec((1, hd, 4 * hd), lambda i, h: (h, 0, 0)),
                  vec, vec, vec],
        out_specs=(blk, blk),
        compiler_params=_params(("parallel", "parallel")),
        name="lru_gates",
    )(xc, w_cat, gate_a_b, gate_x_b, lam)


def _scan_kernel(a_ref, u_ref, o_ref, h_ref, *, seqs):
    d = pl.program_id(0)
    c = pl.program_id(2)
    cc = jnp.where(d == 0, c, seqs.nc - 1 - c)
    start = jnp.where(d == 0, seqs.is_first(cc), seqs.is_last(cc))
    tc = seqs.chunk

    @pl.when(start)
    def _():
        h_ref[...] = jnp.zeros_like(h_ref)

    def group(g, h):
        for s in range(SUBLANES):
            t = g * SUBLANES + s
            row = jnp.where(d == 0, t, tc - 1 - t)
            h = a_ref[pl.ds(row, 1), :] * h + u_ref[pl.ds(row, 1), :]
            o_ref[pl.ds(row, 1), :] = h
        return h

    h_ref[...] = lax.fori_loop(0, tc // SUBLANES, group, h_ref[...])


def lru_scan(a, u, seqs):
    _, m, d_rnn = a.shape
    tc = seqs.chunk
    cb = _pick(d_rnn, 1024, LANES)

    def idx(d, j, c):
        return (d, jnp.where(d == 0, c, seqs.nc - 1 - c), j)

    blk = pl.BlockSpec((None, tc, cb), idx)
    return pl.pallas_call(
        functools.partial(_scan_kernel, seqs=seqs),
        out_shape=jax.ShapeDtypeStruct((2, m, d_rnn), F32),
        grid=(2, d_rnn // cb, seqs.nc),
        in_specs=[blk, blk],
        out_specs=blk,
        scratch_shapes=[pltpu.VMEM((1, cb), F32)],
        compiler_params=_params(("arbitrary", "arbitrary", "arbitrary")),
        name="lru_scan",
    )(a, u)


def _gate_out_kernel(g_ref, h_ref, o_ref):
    o_ref[...] = (jax.nn.gelu(g_ref[...]) * (h_ref[0] + h_ref[1])).astype(o_ref.dtype)


def gate_out(proj, hs, col0):
    _, m, d_rnn = hs.shape
    tm = _pick(m, 512, SUBLANES)
    cb = _pick(math.gcd(col0, d_rnn), 1024, LANES)
    return pl.pallas_call(
        _gate_out_kernel,
        out_shape=jax.ShapeDtypeStruct((m, d_rnn), BF16),
        grid=(m // tm, d_rnn // cb),
        in_specs=[pl.BlockSpec((tm, cb), lambda i, j: (i, col0 // cb + j)),
                  pl.BlockSpec((2, tm, cb), lambda i, j: (0, i, j))],
        out_specs=pl.BlockSpec((tm, cb), lambda i, j: (i, j)),
        compiler_params=_params(("parallel", "parallel")),
        name="gate_out",
    )(proj, hs)


def _attn_kernel(sink_ref, q_ref, kp_ref, kc_ref, kn_ref, vp_ref, vc_ref, vn_ref, o_ref, *,
                 seqs, n_kv, group, hd, slopes):
    c = pl.program_id(0)
    k_lo = jnp.where(seqs.is_first(c), 0, -BLOCK)
    k_hi = jnp.where(seqs.is_last(c), BLOCK, 2 * BLOCK)
    q_pos = lax.broadcasted_iota(jnp.int32, (BLOCK, 3 * BLOCK), 0)
    k_rel = lax.broadcasted_iota(jnp.int32, (BLOCK, 3 * BLOCK), 1) - BLOCK
    dist = jnp.abs(q_pos - k_rel)
    valid = jnp.logical_and(dist <= WINDOW, jnp.logical_and(k_rel >= k_lo, k_rel < k_hi))
    dist_f = dist.astype(F32)
    scale = hd ** -0.5
    for kv in range(n_kv):
        ks = slice(kv * hd, (kv + 1) * hd)
        k3 = jnp.concatenate([kp_ref[:, ks], kc_ref[:, ks], kn_ref[:, ks]], axis=0).astype(BF16)
        v3 = jnp.concatenate([vp_ref[:, ks], vc_ref[:, ks], vn_ref[:, ks]], axis=0).astype(BF16)
        for g in range(group):
            h = kv * group + g
            q = q_ref[:, h * hd:(h + 1) * hd].astype(BF16)
            s = lax.dot_general(q, k3, (((1,), (1,)), ((), ())), preferred_element_type=F32) * scale
            s = jnp.where(valid, s - slopes[h] * dist_f, NEG_INF)
            sink = sink_ref[h]
            m = jnp.maximum(jnp.max(s, axis=-1, keepdims=True), sink)
            p = jnp.exp(s - m)
            denom = jnp.sum(p, axis=-1, keepdims=True) + jnp.exp(sink - m)
            probs = (p / denom).astype(BF16)
            o_ref[:, h * hd:(h + 1) * hd] = jnp.dot(
                probs, v3, preferred_element_type=F32).astype(o_ref.dtype)


def window_attention(qkv, sink, seqs, n_q, n_kv, hd):
    m = qkv.shape[0]
    assert seqs.chunk == BLOCK
    group = n_q // n_kv
    dq = n_q * hd
    dkv = n_kv * hd
    assert dq % dkv == 0
    kcol = dq // dkv
    vcol = kcol + 1
    last = seqs.nc - 1
    slopes = tuple(2.0 ** (-8.0 * (h + 1.0) / n_q) for h in range(n_q))

    def kv_specs(col):
        return [pl.BlockSpec((BLOCK, dkv), lambda c: (jnp.maximum(c - 1, 0), col)),
                pl.BlockSpec((BLOCK, dkv), lambda c: (c, col)),
                pl.BlockSpec((BLOCK, dkv), lambda c: (jnp.minimum(c + 1, last), col))]

    return pl.pallas_call(
        functools.partial(_attn_kernel, seqs=seqs, n_kv=n_kv, group=group, hd=hd, slopes=slopes),
        out_shape=jax.ShapeDtypeStruct((m, dq), BF16),
        grid=(seqs.nc,),
        in_specs=[pl.BlockSpec(memory_space=pltpu.SMEM),
                  pl.BlockSpec((BLOCK, dq), lambda c: (c, 0))] + kv_specs(kcol) + kv_specs(vcol),
        out_specs=pl.BlockSpec((BLOCK, dq), lambda c: (c, 0)),
        compiler_params=_params(("parallel",)),
        name="window_attention",
    )(sink, qkv, qkv, qkv, qkv, qkv, qkv, qkv)


def _router_kernel(x_ref, g_ref, wr_ref, hn_ref, aff_ref):
    x = x_ref[...]
    ms = jnp.mean(x * x, axis=-1, keepdims=True)
    hn = x * lax.rsqrt(ms + EPS) * g_ref[...]
    hn_ref[...] = hn
    logits = lax.dot_general(wr_ref[...], hn.astype(BF16), (((1,), (1,)), ((), ())),
                             preferred_element_type=F32)
    mx = jnp.max(logits, axis=0, keepdims=True)
    p = jnp.exp(logits - mx)
    aff_ref[...] = p / jnp.sum(p, axis=0, keepdims=True)


def router(x, g, w_router_t):
    m, d = x.shape
    e = w_router_t.shape[0]
    tm = _pick(m, 512, LANES)
    return pl.pallas_call(
        _router_kernel,
        out_shape=(jax.ShapeDtypeStruct((m, d), F32), jax.ShapeDtypeStruct((e, m), F32)),
        grid=(m // tm,),
        in_specs=[pl.BlockSpec((tm, d), lambda i: (i, 0)),
                  pl.BlockSpec((1, d), lambda i: (0, 0)),
                  pl.BlockSpec((e, d), lambda i: (0, 0))],
        out_specs=(pl.BlockSpec((tm, d), lambda i: (i, 0)),
                   pl.BlockSpec((e, tm), lambda i: (0, i))),
        compiler_params=_params(("parallel",)),
        name="router",
    )(x, g.reshape(1, d), w_router_t)


def _total(x):
    return jnp.sum(jnp.sum(x, axis=0, keepdims=True), axis=1, keepdims=True)


def _split3(x):
    x1 = x.astype(BF16)
    r1 = x - x1.astype(F32)
    x2 = r1.astype(BF16)
    x3 = (r1 - x2.astype(F32)).astype(BF16)
    return x1, x2, x3


def _select_kernel(aff_ref, idx_ref, gate_ref, *, cap, n_tok):
    grp = pl.program_id(1)
    a = aff_ref[...]
    rows = a.shape[0]
    bits = pltpu.bitcast(a, jnp.int32)
    thr = jnp.zeros((1, 1), jnp.int32)
    for b in range(30, -1, -1):
        cand = thr | (1 << b)
        cnt = _total((bits >= cand).astype(jnp.int32))
        thr = jnp.where(cnt >= cap, cand, thr)
    gt = bits > thr
    eq = bits == thr
    need = (cap - _total(gt.astype(jnp.int32))).astype(F32)

    def iota(shape, dim):
        return lax.broadcasted_iota(jnp.int32, shape, dim)

    def tri(n, keep):
        return keep(iota((n, n), 0), iota((n, n), 1)).astype(BF16)

    def dot(x, y):
        return jnp.dot(x, y, preferred_element_type=F32)

    def dot_nt(x, y):
        return lax.dot_general(x, y, (((1,), (1,)), ((), ())), preferred_element_type=F32)

    upper = tri(LANES, lambda i, j: i <= j)
    lower = tri(LANES, lambda i, j: j <= i)
    before = tri(rows, lambda i, j: j < i)
    upto = tri(rows, lambda i, j: j <= i)
    eye = tri(LANES, lambda i, j: i == j)

    eq_b = eq.astype(BF16)
    eq_rank = dot(eq_b, upper) + jnp.sum(dot(before, eq_b), axis=1, keepdims=True)
    sel = jnp.logical_or(gt, jnp.logical_and(eq, eq_rank <= need))
    s = sel.astype(BF16)

    s_t = dot_nt(eye, s).astype(BF16)
    p_t = dot(lower, s_t).astype(BF16)
    row_tot = jnp.sum(sel.astype(F32), axis=1, keepdims=True)
    c_in = jnp.sum(dot(upto, s), axis=1, keepdims=True)
    c_ex = c_in - row_tot
    j = iota((1, cap), 1).astype(F32)
    r_j = jnp.sum((c_in <= j).astype(F32), axis=0, keepdims=True)
    g_t = (iota((rows, cap), 0).astype(F32) == r_j)
    g_b = g_t.astype(BF16)
    p_row = dot(p_t, g_b)
    k = j - jnp.sum(jnp.where(g_t, c_ex, 0.0), axis=0, keepdims=True)
    c_j = jnp.sum((p_row <= k).astype(F32), axis=0, keepdims=True)
    idx_ref[...] = (r_j * LANES + c_j).astype(jnp.int32) + grp * n_tok

    a_row = jnp.zeros((LANES, cap), F32)
    for part in _split3(a):
        a_row = a_row + dot(dot_nt(eye, part).astype(BF16), g_b)
    hit = iota((LANES, cap), 0).astype(F32) == c_j
    gate = jnp.sum(jnp.where(hit, a_row, 0.0), axis=0, keepdims=True)
    gate_ref[...] = jnp.transpose(jnp.broadcast_to(gate, (LANES, cap)))


def select_tokens(aff, n_groups, cap):
    e, m = aff.shape
    n_tok = m // n_groups
    rows = n_tok // LANES
    aff4 = aff.reshape(e, n_groups, rows, LANES)
    idx, gate = pl.pallas_call(
        functools.partial(_select_kernel, cap=cap, n_tok=n_tok),
        out_shape=(jax.ShapeDtypeStruct((e * n_groups, 1, cap), jnp.int32),
                   jax.ShapeDtypeStruct((e * n_groups * cap, LANES), F32)),
        grid=(e, n_groups),
        in_specs=[pl.BlockSpec((None, None, rows, LANES), lambda i, g: (i, g, 0, 0))],
        out_specs=(pl.BlockSpec((None, 1, cap), lambda i, g: (i * n_groups + g, 0, 0)),
                   pl.BlockSpec((cap, LANES), lambda i, g: (i * n_groups + g, 0))),
        compiler_params=_params(("parallel", "parallel")),
        name="select_tokens",
    )(aff4)
    return idx, gate


def _ffn_kernel(idx_hbm, hn_hbm, gate_ref, wg_ref, wu_ref, wd_ref, x_hbm, o_hbm,
                idx_smem, rows_ref, xb_ref, acc_ref, sems, *, tm, n_f):
    del x_hbm
    t = pl.program_id(0)
    f = pl.program_id(1)
    sub = idx_smem.shape[1]

    def row_copies(src_of, dst_of, sem):
        def start(j, carry):
            tok = idx_smem[j // sub, j % sub]
            pltpu.make_async_copy(src_of(tok, j), dst_of(tok, j), sem).start()
            return carry
        lax.fori_loop(0, tm, start, 0)

    def wait_rows(sem, to_hbm=False):
        if to_hbm:
            pltpu.make_async_copy(rows_ref, o_hbm.at[pl.ds(0, tm)], sem).wait()
        else:
            pltpu.make_async_copy(hn_hbm.at[pl.ds(0, tm)], rows_ref, sem).wait()

    @pl.when(f == 0)
    def _():
        cp = pltpu.make_async_copy(idx_hbm.at[t], idx_smem, sems.at[0])
        cp.start()
        cp.wait()
        row_copies(lambda tok, j: hn_hbm.at[pl.ds(tok, 1)],
                   lambda tok, j: rows_ref.at[pl.ds(j, 1)], sems.at[1])
        wait_rows(sems.at[1])
        xb_ref[...] = rows_ref[...].astype(BF16)
        acc_ref[...] = jnp.zeros_like(acc_ref)

    xb = xb_ref[...]
    hg = jnp.dot(xb, wg_ref[0], preferred_element_type=F32)
    hu = jnp.dot(xb, wu_ref[0], preferred_element_type=F32)
    h = (jax.nn.silu(hg) * hu).astype(BF16)
    acc_ref[...] += jnp.dot(h, wd_ref[0], preferred_element_type=F32)

    @pl.when(f == n_f - 1)
    def _():
        row_copies(lambda tok, j: o_hbm.at[pl.ds(tok, 1)],
                   lambda tok, j: rows_ref.at[pl.ds(j, 1)], sems.at[1])
        wait_rows(sems.at[1])
        rows_ref[...] = rows_ref[...] + acc_ref[...] * gate_ref[:, 0:1]
        row_copies(lambda tok, j: rows_ref.at[pl.ds(j, 1)],
                   lambda tok, j: o_hbm.at[pl.ds(tok, 1)], sems.at[2])
        wait_rows(sems.at[2], to_hbm=True)


def moe_ffn(x, hn, idx, gate, w_gate, w_up, w_down, n_groups, cap):
    m, d = x.shape
    e, _, d_f = w_gate.shape
    rows_per_expert = n_groups * cap
    tm = _pick(rows_per_expert, 512, 2 * SUBLANES)
    fc = _pick(d_f, 256, LANES)
    n_f = d_f // fc
    tiles = rows_per_expert // tm
    sub = tm // SUBLANES
    idx_t = idx.reshape(e * tiles, SUBLANES, sub)
    return pl.pallas_call(
        functools.partial(_ffn_kernel, tm=tm, n_f=n_f),
        out_shape=jax.ShapeDtypeStruct((m, d), F32),
        grid=(e * tiles, n_f),
        in_specs=[pl.BlockSpec(memory_space=pl.ANY),
                  pl.BlockSpec(memory_space=pl.ANY),
                  pl.BlockSpec((tm, LANES), lambda t, f: (t, 0)),
                  pl.BlockSpec((1, d, fc), lambda t, f: (t // tiles, 0, f)),
                  pl.BlockSpec((1, d, fc), lambda t, f: (t // tiles, 0, f)),
                  pl.BlockSpec((1, fc, d), lambda t, f: (t // tiles, f, 0)),
                  pl.BlockSpec(memory_space=pl.ANY)],
        out_specs=pl.BlockSpec(memory_space=pl.ANY),
        scratch_shapes=[pltpu.SMEM((SUBLANES, sub), jnp.int32),
                        pltpu.VMEM((tm, d), F32),
                        pltpu.VMEM((tm, d), BF16),
                        pltpu.VMEM((tm, d), F32),
                        pltpu.SemaphoreType.DMA((3,))],
        input_output_aliases={6: 0},
        compiler_params=_params(("arbitrary", "arbitrary")),
        name="moe_ffn",
    )(idx_t, hn, gate, w_gate, w_up, w_down, x)


def expert_choice_moe(x, norm_g, w_router, w_gate, w_up, w_down, n_groups):
    m = x.shape[0]
    e = w_router.shape[1]
    cap = max(1, CAPACITY_FACTOR * (m // n_groups) // e)
    hn, aff = router(x, norm_g, jnp.transpose(w_router).astype(BF16))
    idx, gate = select_tokens(aff, n_groups, cap)
    return moe_ffn(x, hn, idx, gate, w_gate.astype(BF16), w_up.astype(BF16),
                   w_down.astype(BF16), n_groups, cap)


def kernel(x_prompt, x_sample, mix_norm, ffn_norm, final_norm, ab_w_in, ab_conv_a, ab_conv_b,
           ab_conv_b_bias, ab_gate_a_w, ab_gate_a_b, ab_gate_x_w, ab_gate_x_b, ab_lambda, ab_w_out,
           attn_w_qkv, attn_w_o, attn_sink, moe_w_router, moe_w_gate, moe_w_up, moe_w_down):
    n0, t0, d = x_prompt.shape
    n1, t1, _ = x_sample.shape
    assert n0 * t0 == n1 * t1, "expert capacity is per group: groups must hold equally many tokens"
    depth = mix_norm.shape[0]
    d_conv = ab_conv_a.shape[-1]
    d_rnn = ab_conv_b.shape[-1]
    n_q = attn_sink.shape[-1]
    hd = attn_w_o.shape[1] // n_q
    n_kv = (attn_w_qkv.shape[-1] // hd - n_q) // 2
    seqs_conv = _Seqs(n0, t0, n1, t1, _pick(math.gcd(t0, t1), 512, SUBLANES))
    seqs_attn = _Seqs(n0, t0, n1, t1, BLOCK)

    x = jnp.concatenate([x_prompt.reshape(n0 * t0, d), x_sample.reshape(n1 * t1, d)], axis=0)
    for layer in range(depth):
        j = layer // 2
        hn = rmsnorm(x, mix_norm[layer], BF16)
        if layer % 2 == 0:
            proj = matmul([hn], [ab_w_in[j].astype(BF16)])
            y_a = conv_a(proj, ab_conv_a[j], seqs_conv, d_conv)
            xc = conv_b(proj, ab_conv_b[j], ab_conv_b_bias[j], seqs_conv, 3 * d_conv + d_rnn, d_rnn)
            w_cat = jnp.concatenate([ab_gate_a_w[j, 0], ab_gate_x_w[j, 0],
                                     ab_gate_a_w[j, 1], ab_gate_x_w[j, 1]], axis=-1).astype(BF16)
            a, u = lru_gates(xc, w_cat, ab_gate_a_b[j], ab_gate_x_b[j], ab_lambda[j])
            hs = lru_scan(a, u, seqs_conv)
            y_b = gate_out(proj, hs, 3 * d_conv)
            w_out = ab_w_out[j].astype(BF16)
            x = matmul([y_a, y_b], [w_out[:d_conv], w_out[d_conv:]], res=x, bn=512)
        else:
            qkv = matmul([hn], [attn_w_qkv[j].astype(BF16)])
            att = window_attention(qkv, attn_sink[j], seqs_attn, n_q, n_kv, hd)
            x = matmul([att], [attn_w_o[j].astype(BF16)], res=x, bn=512)
        x = expert_choice_moe(x, ffn_norm[layer], moe_w_router[layer], moe_w_gate[layer],
                              moe_w_up[layer], moe_w_down[layer], n_groups=2)
    y = rmsnorm(x, final_norm, F32)
    return (y[:n0 * t0].reshape(n0, t0, d), y[n0 * t0:].reshape(n1, t1, d))
```

```python
import functools
import math

import jax
import jax.numpy as jnp
from jax import lax
from jax.experimental import pallas as pl
from jax.experimental.pallas import tpu as pltpu

EPS = 1e-6
LRU_C = 8.0
WINDOW = 128
BLOCK = 128
NEG_INF = -1e30
CAPACITY_FACTOR = 2

LANES = 128
SUBLANES = 8
VMEM_LIMIT_BYTES = 56 * 1024 * 1024

F32 = jnp.float32
BF16 = jnp.bfloat16


def _pick(n, target, quantum):
    if n <= target:
        return n
    t = (target // quantum) * quantum
    while t >= quantum:
        if n % t == 0:
            return t
        t -= quantum
    return n


def _params(sem, vmem=VMEM_LIMIT_BYTES):
    return pltpu.CompilerParams(dimension_semantics=sem, vmem_limit_bytes=vmem)


class _Seqs:
    def __init__(self, n0, t0, n1, t1, chunk):
        assert t0 % chunk == 0 and t1 % chunk == 0
        self.chunk = chunk
        self.cps0 = t0 // chunk
        self.cps1 = t1 // chunk
        self.nc0 = n0 * self.cps0
        self.nc = self.nc0 + n1 * self.cps1

    def is_first(self, c):
        return jnp.where(c < self.nc0, c % self.cps0 == 0, (c - self.nc0) % self.cps1 == 0)

    def is_last(self, c):
        return jnp.where(c < self.nc0, c % self.cps0 == self.cps0 - 1,
                         (c - self.nc0) % self.cps1 == self.cps1 - 1)


def _rmsnorm_kernel(x_ref, g_ref, o_ref):
    x = x_ref[...]
    ms = jnp.mean(x * x, axis=-1, keepdims=True)
    o_ref[...] = (x * lax.rsqrt(ms + EPS) * g_ref[...]).astype(o_ref.dtype)


def rmsnorm(x, g, out_dtype, row0=0, rows=None):
    d = x.shape[1]
    m = x.shape[0] if rows is None else rows
    tm = _pick(math.gcd(m, row0), 512, SUBLANES)
    blk0 = row0 // tm
    return pl.pallas_call(
        _rmsnorm_kernel,
        out_shape=jax.ShapeDtypeStruct((m, d), out_dtype),
        grid=(m // tm,),
        in_specs=[pl.BlockSpec((tm, d), lambda i: (blk0 + i, 0)),
                  pl.BlockSpec((1, d), lambda i: (0, 0))],
        out_specs=pl.BlockSpec((tm, d), lambda i: (i, 0)),
        compiler_params=_params(("parallel",)),
        name="rmsnorm",
    )(x, g.reshape(1, d))


def _matmul_kernel(*refs, n_pairs, has_res):
    xs = refs[:n_pairs]
    ws = refs[n_pairs:2 * n_pairs]
    o_ref = refs[-1]
    acc = jnp.dot(xs[0][...], ws[0][...], preferred_element_type=F32)
    for x_ref, w_ref in zip(xs[1:], ws[1:]):
        acc = acc + jnp.dot(x_ref[...], w_ref[...], preferred_element_type=F32)
    if has_res:
        acc = acc + refs[2 * n_pairs][...]
    o_ref[...] = acc.astype(o_ref.dtype)


def matmul(xs, ws, res=None, out_dtype=F32, bm=1024, bn=1024):
    m = xs[0].shape[0]
    n = ws[0].shape[1]
    bm = _pick(m, bm, SUBLANES)
    bn = _pick(n, bn, LANES)
    in_specs = [pl.BlockSpec((bm, x.shape[1]), lambda i, j: (i, 0)) for x in xs]
    in_specs += [pl.BlockSpec((w.shape[0], bn), lambda i, j: (0, j)) for w in ws]
    args = list(xs) + list(ws)
    if res is not None:
        in_specs.append(pl.BlockSpec((bm, bn), lambda i, j: (i, j)))
        args.append(res)
    return pl.pallas_call(
        functools.partial(_matmul_kernel, n_pairs=len(xs), has_res=res is not None),
        out_shape=jax.ShapeDtypeStruct((m, n), out_dtype),
        grid=(m // bm, n // bn),
        in_specs=in_specs,
        out_specs=pl.BlockSpec((bm, bn), lambda i, j: (i, j)),
        compiler_params=_params(("parallel", "parallel")),
        name="matmul",
    )(*args)


def _halo_specs(seqs, cb, col0):
    tc = seqs.chunk
    r8 = tc // SUBLANES
    last8 = seqs.nc * r8 - 1
    main = pl.BlockSpec((tc, cb), lambda c, j: (c, col0 + j))
    prev = pl.BlockSpec((SUBLANES, cb), lambda c, j: (jnp.maximum(c * r8 - 1, 0), col0 + j))
    nxt = pl.BlockSpec((SUBLANES, cb), lambda c, j: (jnp.minimum((c + 1) * r8, last8), col0 + j))
    return [prev, main, nxt]


def _extended(prev, main, nxt, first, last):
    prev = jnp.where(first, 0.0, prev)
    nxt = jnp.where(last, 0.0, nxt)
    return jnp.concatenate([prev, main, nxt], axis=0)


def _shifted(ext, offset, tc):
    n = ext.shape[0]
    if offset == 0:
        return ext[SUBLANES:SUBLANES + tc]
    return pltpu.roll(ext, (-offset) % n, axis=0)[SUBLANES:SUBLANES + tc]


def _conv_a_kernel(cp_ref, c_ref, cn_ref, vp_ref, v_ref, vn_ref, b_ref, w_ref, o_ref, *, seqs):
    c = pl.program_id(0)
    first, last = seqs.is_first(c), seqs.is_last(c)
    tc = seqs.chunk
    z = _extended(cp_ref[...] * vp_ref[...], c_ref[...] * v_ref[...], cn_ref[...] * vn_ref[...],
                  first, last)
    w = w_ref[...]
    y = (w[0:1] * _shifted(z, -1, tc) + w[1:2] * _shifted(z, 0, tc) + w[2:3] * _shifted(z, 1, tc))
    o_ref[...] = (b_ref[...] * y).astype(o_ref.dtype)


def conv_a(proj, conv_w, seqs, d_conv):
    m = proj.shape[0]
    tc = seqs.chunk
    cb = _pick(d_conv, 512, LANES)
    nb = d_conv // cb
    specs = _halo_specs(seqs, cb, nb) + _halo_specs(seqs, cb, 2 * nb)
    specs.append(pl.BlockSpec((tc, cb), lambda c, j: (c, j)))
    specs.append(pl.BlockSpec((conv_w.shape[0], cb), lambda c, j: (0, j)))
    return pl.pallas_call(
        functools.partial(_conv_a_kernel, seqs=seqs),
        out_shape=jax.ShapeDtypeStruct((m, d_conv), BF16),
        grid=(seqs.nc, nb),
        in_specs=specs,
        out_specs=pl.BlockSpec((tc, cb), lambda c, j: (c, j)),
        compiler_params=_params(("parallel", "parallel")),
        name="conv_a",
    )(proj, proj, proj, proj, proj, proj, proj, conv_w)


def _conv_b_kernel(xp_ref, x_ref, xn_ref, w_ref, b_ref, o_ref, *, seqs):
    c = pl.program_id(0)
    tc = seqs.chunk
    x = _extended(xp_ref[...], x_ref[...], xn_ref[...], seqs.is_first(c), seqs.is_last(c))
    w = w_ref[...]
    y = (w[0:1] * _shifted(x, -2, tc) + w[1:2] * _shifted(x, -1, tc)
         + w[2:3] * _shifted(x, 0, tc) + w[3:4] * _shifted(x, 1, tc))
    o_ref[...] = y + b_ref[...]


def conv_b(proj, conv_w, bias, seqs, col0, d_rnn):
    m = proj.shape[0]
    tc = seqs.chunk
    cb = _pick(math.gcd(col0, d_rnn), 512, LANES)
    nb = d_rnn // cb
    specs = _halo_specs(seqs, cb, col0 // cb)
    specs.append(pl.BlockSpec((conv_w.shape[0], cb), lambda c, j: (0, j)))
    specs.append(pl.BlockSpec((1, cb), lambda c, j: (0, j)))
    return pl.pallas_call(
        functools.partial(_conv_b_kernel, seqs=seqs),
        out_shape=jax.ShapeDtypeStruct((m, d_rnn), F32),
        grid=(seqs.nc, nb),
        in_specs=specs,
        out_specs=pl.BlockSpec((tc, cb), lambda c, j: (c, j)),
        compiler_params=_params(("parallel", "parallel")),
        name="conv_b",
    )(proj, proj, proj, conv_w, bias.reshape(1, d_rnn))


def _gates_kernel(x_ref, w_ref, ba_ref, bx_ref, lam_ref, a_ref, u_ref, *, hd):
    x = x_ref[...]
    z = jnp.dot(x.astype(BF16), w_ref[0], preferred_element_type=F32)
    def sigmoid(v):
        return 0.5 * jnp.tanh(0.5 * v) + 0.5

    for d in range(2):
        r = sigmoid(z[:, (2 * d) * hd:(2 * d + 1) * hd] + ba_ref[d:d + 1, :])
        i = sigmoid(z[:, (2 * d + 1) * hd:(2 * d + 2) * hd] + bx_ref[d:d + 1, :])
        log_a = -LRU_C * r * jax.nn.softplus(-lam_ref[d:d + 1, :])
        a = jnp.exp(log_a)
        a_ref[d] = a
        u_ref[d] = jnp.sqrt(-jnp.tanh(log_a) * (1.0 + a * a)) * (i * x)


def lru_gates(xc, w_cat, gate_a_b, gate_x_b, lam):
    m, d_rnn = xc.shape
    heads, hd, _ = w_cat.shape
    tm = _pick(m, 1024, SUBLANES)
    out = jax.ShapeDtypeStruct((2, m, d_rnn), F32)
    vec = pl.BlockSpec((2, hd), lambda i, h: (0, h))
    blk = pl.BlockSpec((2, tm, hd), lambda i, h: (0, i, h))
    return pl.pallas_call(
        functools.partial(_gates_kernel, hd=hd),
        out_shape=(out, out),
        grid=(m // tm, heads),
        in_specs=[pl.BlockSpec((tm, hd), lambda i, h: (i, h)),
                  pl.BlockSpec((1, hd, 4 * hd), lambda i, h: (h, 0, 0)),
                  vec, vec, vec],
        out_specs=(blk, blk),
        compiler_params=_params(("parallel", "parallel")),
        name="lru_gates",
    )(xc, w_cat, gate_a_b, gate_x_b, lam)


def _scan_kernel(a_ref, u_ref, o_ref, h_ref, *, seqs):
    d = pl.program_id(0)
    c = pl.program_id(2)
    cc = jnp.where(d == 0, c, seqs.nc - 1 - c)
    start = jnp.where(d == 0, seqs.is_first(cc), seqs.is_last(cc))
    tc = seqs.chunk

    @pl.when(start)
    def _():
        h_ref[...] = jnp.zeros_like(h_ref)

    def group(g, h):
        for s in range(SUBLANES):
            t = g * SUBLANES + s
            row = jnp.where(d == 0, t, tc - 1 - t)
            h = a_ref[pl.ds(row, 1), :] * h + u_ref[pl.ds(row, 1), :]
            o_ref[pl.ds(row, 1), :] = h
        return h

    h_ref[...] = lax.fori_loop(0, tc // SUBLANES, group, h_ref[...])


def lru_scan(a, u, seqs):
    _, m, d_rnn = a.shape
    tc = seqs.chunk
    cb = _pick(d_rnn, 1024, LANES)

    def idx(d, j, c):
        return (d, jnp.where(d == 0, c, seqs.nc - 1 - c), j)

    blk = pl.BlockSpec((None, tc, cb), idx)
    return pl.pallas_call(
        functools.partial(_scan_kernel, seqs=seqs),
        out_shape=jax.ShapeDtypeStruct((2, m, d_rnn), F32),
        grid=(2, d_rnn // cb, seqs.nc),
        in_specs=[blk, blk],
        out_specs=blk,
        scratch_shapes=[pltpu.VMEM((1, cb), F32)],
        compiler_params=_params(("arbitrary", "arbitrary", "arbitrary")),
        name="lru_scan",
    )(a, u)


def _gate_out_kernel(g_ref, h_ref, o_ref):
    o_ref[...] = (jax.nn.gelu(g_ref[...]) * (h_ref[0] + h_ref[1])).astype(o_ref.dtype)


def gate_out(proj, hs, col0):
    _, m, d_rnn = hs.shape
    tm = _pick(m, 512, SUBLANES)
    cb = _pick(math.gcd(col0, d_rnn), 1024, LANES)
    return pl.pallas_call(
        _gate_out_kernel,
        out_shape=jax.ShapeDtypeStruct((m, d_rnn), BF16),
        grid=(m // tm, d_rnn // cb),
        in_specs=[pl.BlockSpec((tm, cb), lambda i, j: (i, col0 // cb + j)),
                  pl.BlockSpec((2, tm, cb), lambda i, j: (0, i, j))],
        out_specs=pl.BlockSpec((tm, cb), lambda i, j: (i, j)),
        compiler_params=_params(("parallel", "parallel")),
        name="gate_out",
    )(proj, hs)


def _attn_kernel(sink_ref, q_ref, kp_ref, kc_ref, kn_ref, vp_ref, vc_ref, vn_ref, o_ref, *,
                 seqs, n_kv, group, hd, slopes):
    c = pl.program_id(0)
    k_lo = jnp.where(seqs.is_first(c), 0, -BLOCK)
    k_hi = jnp.where(seqs.is_last(c), BLOCK, 2 * BLOCK)
    rows = group * BLOCK
    row = lax.broadcasted_iota(jnp.int32, (rows, 3 * BLOCK), 0)
    k_rel = lax.broadcasted_iota(jnp.int32, (rows, 3 * BLOCK), 1) - BLOCK
    dist = jnp.abs((row & (BLOCK - 1)) - k_rel)
    valid = jnp.logical_and(dist <= WINDOW, jnp.logical_and(k_rel >= k_lo, k_rel < k_hi))
    dist_f = dist.astype(F32)
    g_of_row = lax.shift_right_logical(lax.broadcasted_iota(jnp.int32, (rows, 1), 0),
                                       BLOCK.bit_length() - 1)
    scale = hd ** -0.5

    def per_row(values):
        col = jnp.full((rows, 1), values[group - 1], F32)
        for g in range(group - 2, -1, -1):
            col = jnp.where(g_of_row == g, values[g], col)
        return col

    for kv in range(n_kv):
        ks = slice(kv * hd, (kv + 1) * hd)
        heads = [kv * group + g for g in range(group)]
        k3 = jnp.concatenate([kp_ref[:, ks], kc_ref[:, ks], kn_ref[:, ks]], axis=0)
        v3 = jnp.concatenate([vp_ref[:, ks], vc_ref[:, ks], vn_ref[:, ks]], axis=0)
        q = jnp.concatenate([q_ref[:, h * hd:(h + 1) * hd] for h in heads], axis=0)
        slope = per_row([slopes[h] for h in heads])
        sink = per_row([sink_ref[h] for h in heads])
        s = lax.dot_general(q, k3, (((1,), (1,)), ((), ())), preferred_element_type=F32) * scale
        s = jnp.where(valid, s - slope * dist_f, NEG_INF)
        m = jnp.maximum(jnp.max(s, axis=-1, keepdims=True), sink)
        p = jnp.exp(s - m)
        denom = jnp.sum(p, axis=-1, keepdims=True) + jnp.exp(sink - m)
        probs = (p * (1.0 / denom)).astype(BF16)
        out = jnp.dot(probs, v3, preferred_element_type=F32).astype(o_ref.dtype)
        for g, h in enumerate(heads):
            o_ref[:, h * hd:(h + 1) * hd] = out[g * BLOCK:(g + 1) * BLOCK]


def window_attention(qkv, sink, seqs, n_q, n_kv, hd):
    m = qkv.shape[0]
    assert seqs.chunk == BLOCK
    group = n_q // n_kv
    dq = n_q * hd
    dkv = n_kv * hd
    assert dq % dkv == 0
    kcol = dq // dkv
    vcol = kcol + 1
    last = seqs.nc - 1
    slopes = tuple(2.0 ** (-8.0 * (h + 1.0) / n_q) for h in range(n_q))

    def kv_specs(col):
        return [pl.BlockSpec((BLOCK, dkv), lambda c: (jnp.maximum(c - 1, 0), col)),
                pl.BlockSpec((BLOCK, dkv), lambda c: (c, col)),
                pl.BlockSpec((BLOCK, dkv), lambda c: (jnp.minimum(c + 1, last), col))]

    return pl.pallas_call(
        functools.partial(_attn_kernel, seqs=seqs, n_kv=n_kv, group=group, hd=hd, slopes=slopes),
        out_shape=jax.ShapeDtypeStruct((m, dq), BF16),
        grid=(seqs.nc,),
        in_specs=[pl.BlockSpec(memory_space=pltpu.SMEM),
                  pl.BlockSpec((BLOCK, dq), lambda c: (c, 0))] + kv_specs(kcol) + kv_specs(vcol),
        out_specs=pl.BlockSpec((BLOCK, dq), lambda c: (c, 0)),
        compiler_params=_params(("parallel",)),
        name="window_attention",
    )(sink, qkv, qkv, qkv, qkv, qkv, qkv, qkv)


def _router_kernel(x_ref, g_ref, wr_ref, hn_ref, aff_ref):
    x = x_ref[...]
    ms = jnp.mean(x * x, axis=-1, keepdims=True)
    hn = x * lax.rsqrt(ms + EPS) * g_ref[...]
    hn_ref[...] = hn
    logits = lax.dot_general(wr_ref[...], hn.astype(BF16), (((1,), (1,)), ((), ())),
                             preferred_element_type=F32)
    mx = jnp.max(logits, axis=0, keepdims=True)
    p = jnp.exp(logits - mx)
    aff_ref[...] = p / jnp.sum(p, axis=0, keepdims=True)


def router(x, g, w_router_t):
    m, d = x.shape
    e = w_router_t.shape[0]
    tm = _pick(m, 512, LANES)
    return pl.pallas_call(
        _router_kernel,
        out_shape=(jax.ShapeDtypeStruct((m, d), F32), jax.ShapeDtypeStruct((e, m), F32)),
        grid=(m // tm,),
        in_specs=[pl.BlockSpec((tm, d), lambda i: (i, 0)),
                  pl.BlockSpec((1, d), lambda i: (0, 0)),
                  pl.BlockSpec((e, d), lambda i: (0, 0))],
        out_specs=(pl.BlockSpec((tm, d), lambda i: (i, 0)),
                   pl.BlockSpec((e, tm), lambda i: (0, i))),
        compiler_params=_params(("parallel",)),
        name="router",
    )(x, g.reshape(1, d), w_router_t)


def _total(x):
    return jnp.sum(jnp.sum(x, axis=0, keepdims=True), axis=1, keepdims=True)


def _split3(x):
    x1 = x.astype(BF16)
    r1 = x - x1.astype(F32)
    x2 = r1.astype(BF16)
    x3 = (r1 - x2.astype(F32)).astype(BF16)
    return x1, x2, x3


def _select_kernel(aff_ref, idx_ref, gate_ref, *, cap, n_tok):
    grp = pl.program_id(1)
    a = aff_ref[...]
    rows = a.shape[0]
    bits = pltpu.bitcast(a, jnp.int32)
    thr = jnp.zeros((1, 1), jnp.int32)
    for b in range(30, -1, -1):
        cand = thr | (1 << b)
        cnt = _total((bits >= cand).astype(jnp.int32))
        thr = jnp.where(cnt >= cap, cand, thr)
    gt = bits > thr
    eq = bits == thr
    need = (cap - _total(gt.astype(jnp.int32))).astype(F32)

    def iota(shape, dim):
        return lax.broadcasted_iota(jnp.int32, shape, dim)

    def tri(n, keep):
        return keep(iota((n, n), 0), iota((n, n), 1)).astype(BF16)

    def dot(x, y):
        return jnp.dot(x, y, preferred_element_type=F32)

    def dot_nt(x, y):
        return lax.dot_general(x, y, (((1,), (1,)), ((), ())), preferred_element_type=F32)

    upper = tri(LANES, lambda i, j: i <= j)
    lower = tri(LANES, lambda i, j: j <= i)
    before = tri(rows, lambda i, j: j < i)
    upto = tri(rows, lambda i, j: j <= i)
    eye = tri(LANES, lambda i, j: i == j)

    eq_b = eq.astype(BF16)
    eq_rank = dot(eq_b, upper) + jnp.sum(dot(before, eq_b), axis=1, keepdims=True)
    sel = jnp.logical_or(gt, jnp.logical_and(eq, eq_rank <= need))
    s = sel.astype(BF16)

    s_t = dot_nt(eye, s).astype(BF16)
    p_t = dot(lower, s_t).astype(BF16)
    row_tot = jnp.sum(sel.astype(F32), axis=1, keepdims=True)
    c_in = jnp.sum(dot(upto, s), axis=1, keepdims=True)
    c_ex = c_in - row_tot
    j = iota((1, cap), 1).astype(F32)
    r_j = jnp.sum((c_in <= j).astype(F32), axis=0, keepdims=True)
    g_t = (iota((rows, cap), 0).astype(F32) == r_j)
    g_b = g_t.astype(BF16)
    p_row = dot(p_t, g_b)
    k = j - jnp.sum(jnp.where(g_t, c_ex, 0.0), axis=0, keepdims=True)
    c_j = jnp.sum((p_row <= k).astype(F32), axis=0, keepdims=True)
    idx_ref[...] = (r_j * LANES + c_j).astype(jnp.int32) + grp * n_tok

    a_row = jnp.zeros((LANES, cap), F32)
    for part in _split3(a):
        a_row = a_row + dot(dot_nt(eye, part).astype(BF16), g_b)
    hit = iota((LANES, cap), 0).astype(F32) == c_j
    gate = jnp.sum(jnp.where(hit, a_row, 0.0), axis=0, keepdims=True)
    gate_ref[...] = jnp.transpose(jnp.broadcast_to(gate, (LANES, cap)))


def select_tokens(aff, n_groups, cap):
    e, m = aff.shape
    n_tok = m // n_groups
    rows = n_tok // LANES
    aff4 = aff.reshape(e, n_groups, rows, LANES)
    idx, gate = pl.pallas_call(
        functools.partial(_select_kernel, cap=cap, n_tok=n_tok),
        out_shape=(jax.ShapeDtypeStruct((e * n_groups, 1, cap), jnp.int32),
                   jax.ShapeDtypeStruct((e * n_groups * cap, LANES), F32)),
        grid=(e, n_groups),
        in_specs=[pl.BlockSpec((None, None, rows, LANES), lambda i, g: (i, g, 0, 0))],
        out_specs=(pl.BlockSpec((None, 1, cap), lambda i, g: (i * n_groups + g, 0, 0)),
                   pl.BlockSpec((cap, LANES), lambda i, g: (i * n_groups + g, 0))),
        compiler_params=_params(("parallel", "parallel")),
        name="select_tokens",
    )(aff4)
    return idx, gate


class _RowDma:
    def __init__(self, idx_hbm, idx_smem, buf, sems, tm):
        self.idx_hbm, self.idx_smem, self.buf, self.sems, self.tm = idx_hbm, idx_smem, buf, sems, tm
        self.sub = idx_smem.shape[2]

    def fetch_idx(self, tile, slot):
        cp = pltpu.make_async_copy(self.idx_hbm.at[tile], self.idx_smem.at[slot], self.sems.at[4])
        cp.start()
        cp.wait()

    def _rows(self, slot, hbm, to_hbm):
        for j in range(self.tm):
            tok = self.idx_smem[slot, j // self.sub, j % self.sub]
            row_hbm = hbm.at[pl.ds(tok, 1)]
            row_vmem = self.buf.at[slot, pl.ds(j, 1)]
            if to_hbm:
                pltpu.make_async_copy(row_vmem, row_hbm, self.sems.at[2 + slot]).start()
            else:
                pltpu.make_async_copy(row_hbm, row_vmem, self.sems.at[slot]).start()

    def gather(self, slot, hbm):
        self._rows(slot, hbm, to_hbm=False)

    def scatter(self, slot, hbm):
        self._rows(slot, hbm, to_hbm=True)

    def wait_gather(self, slot, hbm):
        pltpu.make_async_copy(hbm.at[pl.ds(0, self.tm)], self.buf.at[slot], self.sems.at[slot]).wait()

    def wait_scatter(self, slot, hbm):
        pltpu.make_async_copy(self.buf.at[slot], hbm.at[pl.ds(0, self.tm)],
                              self.sems.at[2 + slot]).wait()


def _row_dma_scratch(tm, d):
    return [pltpu.SMEM((2, SUBLANES, tm // SUBLANES), jnp.int32),
            pltpu.VMEM((2, tm, d), F32),
            pltpu.SemaphoreType.DMA((5,))]


def _gather_kernel(idx_hbm, hn_hbm, o_ref, idx_smem, buf, sems, *, tm, n_steps):
    s = pl.program_id(0)
    dma = _RowDma(idx_hbm, idx_smem, buf, sems, tm)

    @pl.when(s == 0)
    def _():
        dma.fetch_idx(0, 0)
        dma.gather(0, hn_hbm)

    dma.fetch_idx(2 * s + 1, 1)
    dma.gather(1, hn_hbm)
    dma.wait_gather(0, hn_hbm)
    o_ref[0:tm] = buf[0].astype(o_ref.dtype)

    @pl.when(s + 1 < n_steps)
    def _():
        dma.fetch_idx(2 * s + 2, 0)
        dma.gather(0, hn_hbm)

    dma.wait_gather(1, hn_hbm)
    o_ref[tm:2 * tm] = buf[1].astype(o_ref.dtype)


def moe_gather(hn, idx_t, tm):
    d = hn.shape[1]
    n_steps = idx_t.shape[0] // 2
    return pl.pallas_call(
        functools.partial(_gather_kernel, tm=tm, n_steps=n_steps),
        out_shape=jax.ShapeDtypeStruct((2 * n_steps * tm, d), BF16),
        grid=(n_steps,),
        in_specs=[pl.BlockSpec(memory_space=pl.ANY), pl.BlockSpec(memory_space=pl.ANY)],
        out_specs=pl.BlockSpec((2 * tm, d), lambda s: (s, 0)),
        scratch_shapes=_row_dma_scratch(tm, d),
        compiler_params=pltpu.CompilerParams(dimension_semantics=("arbitrary",),
                                             vmem_limit_bytes=VMEM_LIMIT_BYTES),
        name="moe_gather",
    )(idx_t, hn)


def _up_kernel(x_ref, wg_ref, wu_ref, o_ref):
    x = x_ref[...]
    hg = jnp.dot(x, wg_ref[0], preferred_element_type=F32)
    hu = jnp.dot(x, wu_ref[0], preferred_element_type=F32)
    o_ref[...] = (jax.nn.silu(hg) * hu).astype(o_ref.dtype)


def moe_up(xe, w_gate, w_up, rows_per_expert):
    m, d = xe.shape
    d_f = w_gate.shape[2]
    bm = _pick(rows_per_expert, 1024, 2 * SUBLANES)
    bn = _pick(d_f, 512, LANES)
    per = rows_per_expert // bm
    w_spec = pl.BlockSpec((1, d, bn), lambda i, j: (i // per, 0, j))
    return pl.pallas_call(
        _up_kernel,
        out_shape=jax.ShapeDtypeStruct((m, d_f), BF16),
        grid=(m // bm, d_f // bn),
        in_specs=[pl.BlockSpec((bm, d), lambda i, j: (i, 0)), w_spec, w_spec],
        out_specs=pl.BlockSpec((bm, bn), lambda i, j: (i, j)),
        compiler_params=_params(("parallel", "parallel")),
        name="moe_up",
    )(xe, w_gate, w_up)


def _down_kernel(h_ref, wd_ref, gate_ref, o_ref):
    y = jnp.dot(h_ref[...], wd_ref[0], preferred_element_type=F32)
    o_ref[...] = (y * gate_ref[:, 0:1]).astype(o_ref.dtype)


def moe_down(h, w_down, gate, rows_per_expert):
    m, d_f = h.shape
    d = w_down.shape[2]
    bm = _pick(rows_per_expert, 1024, 2 * SUBLANES)
    bn = _pick(d, 1024, LANES)
    per = rows_per_expert // bm
    return pl.pallas_call(
        _down_kernel,
        out_shape=jax.ShapeDtypeStruct((m, d), BF16),
        grid=(m // bm, d // bn),
        in_specs=[pl.BlockSpec((bm, d_f), lambda i, j: (i, 0)),
                  pl.BlockSpec((1, d_f, bn), lambda i, j: (i // per, 0, j)),
                  pl.BlockSpec((bm, LANES), lambda i, j: (i, 0))],
        out_specs=pl.BlockSpec((bm, bn), lambda i, j: (i, j)),
        compiler_params=_params(("parallel", "parallel")),
        name="moe_down",
    )(h, w_down, gate)


def _combine_kernel(idx_hbm, ye_ref, x_hbm, o_hbm, idx_smem, buf, sems, *, tm, n_steps):
    del x_hbm
    s = pl.program_id(0)
    dma = _RowDma(idx_hbm, idx_smem, buf, sems, tm)

    @pl.when(s == 0)
    def _():
        dma.fetch_idx(0, 0)
        dma.gather(0, o_hbm)

    @pl.when(s > 0)
    def _():
        dma.wait_scatter(1, o_hbm)

    dma.fetch_idx(2 * s + 1, 1)
    dma.gather(1, o_hbm)
    for slot in range(2):
        dma.wait_gather(slot, o_hbm)
        buf[slot] = buf[slot] + ye_ref[slot * tm:(slot + 1) * tm].astype(F32)
        dma.scatter(slot, o_hbm)
    dma.wait_scatter(0, o_hbm)

    @pl.when(s + 1 < n_steps)
    def _():
        dma.fetch_idx(2 * s + 2, 0)
        dma.gather(0, o_hbm)

    @pl.when(s == n_steps - 1)
    def _():
        dma.wait_scatter(1, o_hbm)


def moe_combine(x, ye, idx_t, tm):
    m, d = x.shape
    n_steps = idx_t.shape[0] // 2
    return pl.pallas_call(
        functools.partial(_combine_kernel, tm=tm, n_steps=n_steps),
        out_shape=jax.ShapeDtypeStruct((m, d), F32),
        grid=(n_steps,),
        in_specs=[pl.BlockSpec(memory_space=pl.ANY),
                  pl.BlockSpec((2 * tm, d), lambda s: (s, 0)),
                  pl.BlockSpec(memory_space=pl.ANY)],
        out_specs=pl.BlockSpec(memory_space=pl.ANY),
        scratch_shapes=_row_dma_scratch(tm, d),
        input_output_aliases={2: 0},
        compiler_params=pltpu.CompilerParams(dimension_semantics=("arbitrary",),
                                             vmem_limit_bytes=VMEM_LIMIT_BYTES),
        name="moe_combine",
    )(idx_t, ye, x)


def expert_choice_moe(x, norm_g, w_router, w_gate, w_up, w_down, n_groups):
    m = x.shape[0]
    e = w_router.shape[1]
    cap = max(1, CAPACITY_FACTOR * (m // n_groups) // e)
    hn, aff = router(x, norm_g, jnp.transpose(w_router).astype(BF16))
    idx, gate = select_tokens(aff, n_groups, cap)
    tm = _pick(cap, 512, SUBLANES * SUBLANES)
    assert (e * n_groups * cap) % (2 * tm) == 0
    idx_t = idx.reshape(-1, SUBLANES, tm // SUBLANES)
    xe = moe_gather(hn, idx_t, tm)
    h = moe_up(xe, w_gate.astype(BF16), w_up.astype(BF16), n_groups * cap)
    ye = moe_down(h, w_down.astype(BF16), gate, n_groups * cap)
    return moe_combine(x, ye, idx_t, tm)


def kernel(x_prompt, x_sample, mix_norm, ffn_norm, final_norm, ab_w_in, ab_conv_a, ab_conv_b,
           ab_conv_b_bias, ab_gate_a_w, ab_gate_a_b, ab_gate_x_w, ab_gate_x_b, ab_lambda, ab_w_out,
           attn_w_qkv, attn_w_o, attn_sink, moe_w_router, moe_w_gate, moe_w_up, moe_w_down):
    n0, t0, d = x_prompt.shape
    n1, t1, _ = x_sample.shape
    assert n0 * t0 == n1 * t1, "expert capacity is per group: groups must hold equally many tokens"
    depth = mix_norm.shape[0]
    d_conv = ab_conv_a.shape[-1]
    d_rnn = ab_conv_b.shape[-1]
    n_q = attn_sink.shape[-1]
    hd = attn_w_o.shape[1] // n_q
    n_kv = (attn_w_qkv.shape[-1] // hd - n_q) // 2
    seqs_conv = _Seqs(n0, t0, n1, t1, _pick(math.gcd(t0, t1), 512, SUBLANES))
    seqs_attn = _Seqs(n0, t0, n1, t1, BLOCK)

    x = jnp.concatenate([x_prompt.reshape(n0 * t0, d), x_sample.reshape(n1 * t1, d)], axis=0)
    for layer in range(depth):
        j = layer // 2
        hn = rmsnorm(x, mix_norm[layer], BF16)
        if layer % 2 == 0:
            proj = matmul([hn], [ab_w_in[j].astype(BF16)])
            y_a = conv_a(proj, ab_conv_a[j], seqs_conv, d_conv)
            xc = conv_b(proj, ab_conv_b[j], ab_conv_b_bias[j], seqs_conv, 3 * d_conv + d_rnn, d_rnn)
            w_cat = jnp.concatenate([ab_gate_a_w[j, 0], ab_gate_x_w[j, 0],
                                     ab_gate_a_w[j, 1], ab_gate_x_w[j, 1]], axis=-1).astype(BF16)
            a, u = lru_gates(xc, w_cat, ab_gate_a_b[j], ab_gate_x_b[j], ab_lambda[j])
            hs = lru_scan(a, u, seqs_conv)
            y_b = gate_out(proj, hs, 3 * d_conv)
            w_out = ab_w_out[j].astype(BF16)
            x = matmul([y_a, y_b], [w_out[:d_conv], w_out[d_conv:]], res=x, bn=512)
        else:
            qkv = matmul([hn], [attn_w_qkv[j].astype(BF16)], out_dtype=BF16)
            att = window_attention(qkv, attn_sink[j], seqs_attn, n_q, n_kv, hd)
            x = matmul([att], [attn_w_o[j].astype(BF16)], res=x, bn=512)
        x = expert_choice_moe(x, ffn_norm[layer], moe_w_router[layer], moe_w_gate[layer],
                              moe_w_up[layer], moe_w_down[layer], n_groups=2)
    y_prompt = rmsnorm(x, final_norm, F32, row0=0, rows=n0 * t0)
    y_sample = rmsnorm(x, final_norm, F32, row0=n0 * t0, rows=n1 * t1)
    return (y_prompt.reshape(n0, t0, d), y_sample.reshape(n1, t1, d))
```

```python
import functools
import math

import jax
import jax.numpy as jnp
from jax import lax
from jax.experimental import pallas as pl
from jax.experimental.pallas import tpu as pltpu

EPS = 1e-6
LRU_C = 8.0
WINDOW = 128
BLOCK = 128
NEG_INF = -1e30
CAPACITY_FACTOR = 2

LANES = 128
SUBLANES = 8
VMEM_LIMIT_BYTES = 56 * 1024 * 1024

F32 = jnp.float32
BF16 = jnp.bfloat16


def _pick(n, target, quantum):
    if n <= target:
        return n
    t = (target // quantum) * quantum
    while t >= quantum:
        if n % t == 0:
            return t
        t -= quantum
    return n


def _params(sem, vmem=VMEM_LIMIT_BYTES):
    return pltpu.CompilerParams(dimension_semantics=sem, vmem_limit_bytes=vmem)


class _Seqs:
    def __init__(self, n0, t0, n1, t1, chunk):
        assert t0 % chunk == 0 and t1 % chunk == 0
        self.chunk = chunk
        self.cps0 = t0 // chunk
        self.cps1 = t1 // chunk
        self.nc0 = n0 * self.cps0
        self.nc = self.nc0 + n1 * self.cps1

    def is_first(self, c):
        return jnp.where(c < self.nc0, c % self.cps0 == 0, (c - self.nc0) % self.cps1 == 0)

    def is_last(self, c):
        return jnp.where(c < self.nc0, c % self.cps0 == self.cps0 - 1,
                         (c - self.nc0) % self.cps1 == self.cps1 - 1)


def _rmsnorm_kernel(x_ref, g_ref, o_ref):
    x = x_ref[...]
    ms = jnp.mean(x * x, axis=-1, keepdims=True)
    o_ref[...] = (x * lax.rsqrt(ms + EPS) * g_ref[...]).astype(o_ref.dtype)


def rmsnorm(x, g, out_dtype, row0=0, rows=None):
    d = x.shape[1]
    m = x.shape[0] if rows is None else rows
    tm = _pick(math.gcd(m, row0), 512, SUBLANES)
    blk0 = row0 // tm
    return pl.pallas_call(
        _rmsnorm_kernel,
        out_shape=jax.ShapeDtypeStruct((m, d), out_dtype),
        grid=(m // tm,),
        in_specs=[pl.BlockSpec((tm, d), lambda i: (blk0 + i, 0)),
                  pl.BlockSpec((1, d), lambda i: (0, 0))],
        out_specs=pl.BlockSpec((tm, d), lambda i: (i, 0)),
        compiler_params=_params(("parallel",)),
        name="rmsnorm",
    )(x, g.reshape(1, d))


def _matmul_kernel(*refs, n_pairs, has_res):
    xs = refs[:n_pairs]
    ws = refs[n_pairs:2 * n_pairs]
    o_ref = refs[-1]
    acc = jnp.dot(xs[0][...], ws[0][...], preferred_element_type=F32)
    for x_ref, w_ref in zip(xs[1:], ws[1:]):
        acc = acc + jnp.dot(x_ref[...], w_ref[...], preferred_element_type=F32)
    if has_res:
        acc = acc + refs[2 * n_pairs][...]
    o_ref[...] = acc.astype(o_ref.dtype)


def matmul(xs, ws, res=None, out_dtype=F32, bm=1024, bn=1024):
    m = xs[0].shape[0]
    n = ws[0].shape[1]
    bm = _pick(m, bm, SUBLANES)
    bn = _pick(n, bn, LANES)
    in_specs = [pl.BlockSpec((bm, x.shape[1]), lambda i, j: (i, 0)) for x in xs]
    in_specs += [pl.BlockSpec((w.shape[0], bn), lambda i, j: (0, j)) for w in ws]
    args = list(xs) + list(ws)
    if res is not None:
        in_specs.append(pl.BlockSpec((bm, bn), lambda i, j: (i, j)))
        args.append(res)
    return pl.pallas_call(
        functools.partial(_matmul_kernel, n_pairs=len(xs), has_res=res is not None),
        out_shape=jax.ShapeDtypeStruct((m, n), out_dtype),
        grid=(m // bm, n // bn),
        in_specs=in_specs,
        out_specs=pl.BlockSpec((bm, bn), lambda i, j: (i, j)),
        compiler_params=_params(("parallel", "parallel")),
        name="matmul",
    )(*args)


def _halo_specs(seqs, cb, col0):
    tc = seqs.chunk
    r8 = tc // SUBLANES
    last8 = seqs.nc * r8 - 1
    main = pl.BlockSpec((tc, cb), lambda c, j: (c, col0 + j))
    prev = pl.BlockSpec((SUBLANES, cb), lambda c, j: (jnp.maximum(c * r8 - 1, 0), col0 + j))
    nxt = pl.BlockSpec((SUBLANES, cb), lambda c, j: (jnp.minimum((c + 1) * r8, last8), col0 + j))
    return [prev, main, nxt]


def _extended(prev, main, nxt, first, last):
    prev = jnp.where(first, 0.0, prev)
    nxt = jnp.where(last, 0.0, nxt)
    return jnp.concatenate([prev, main, nxt], axis=0)


def _shifted(ext, offset, tc):
    n = ext.shape[0]
    if offset == 0:
        return ext[SUBLANES:SUBLANES + tc]
    return pltpu.roll(ext, (-offset) % n, axis=0)[SUBLANES:SUBLANES + tc]


def _conv_a_kernel(cp_ref, c_ref, cn_ref, vp_ref, v_ref, vn_ref, b_ref, w_ref, o_ref, *, seqs):
    c = pl.program_id(0)
    first, last = seqs.is_first(c), seqs.is_last(c)
    tc = seqs.chunk
    z = _extended(cp_ref[...] * vp_ref[...], c_ref[...] * v_ref[...], cn_ref[...] * vn_ref[...],
                  first, last)
    w = w_ref[...]
    y = (w[0:1] * _shifted(z, -1, tc) + w[1:2] * _shifted(z, 0, tc) + w[2:3] * _shifted(z, 1, tc))
    o_ref[...] = (b_ref[...] * y).astype(o_ref.dtype)


def conv_a(proj, conv_w, seqs, d_conv):
    m = proj.shape[0]
    tc = seqs.chunk
    cb = _pick(d_conv, 512, LANES)
    nb = d_conv // cb
    specs = _halo_specs(seqs, cb, nb) + _halo_specs(seqs, cb, 2 * nb)
    specs.append(pl.BlockSpec((tc, cb), lambda c, j: (c, j)))
    specs.append(pl.BlockSpec((conv_w.shape[0], cb), lambda c, j: (0, j)))
    return pl.pallas_call(
        functools.partial(_conv_a_kernel, seqs=seqs),
        out_shape=jax.ShapeDtypeStruct((m, d_conv), BF16),
        grid=(seqs.nc, nb),
        in_specs=specs,
        out_specs=pl.BlockSpec((tc, cb), lambda c, j: (c, j)),
        compiler_params=_params(("parallel", "parallel")),
        name="conv_a",
    )(proj, proj, proj, proj, proj, proj, proj, conv_w)


def _conv_b_kernel(xp_ref, x_ref, xn_ref, w_ref, b_ref, o_ref, *, seqs):
    c = pl.program_id(0)
    tc = seqs.chunk
    x = _extended(xp_ref[...], x_ref[...], xn_ref[...], seqs.is_first(c), seqs.is_last(c))
    w = w_ref[...]
    y = (w[0:1] * _shifted(x, -2, tc) + w[1:2] * _shifted(x, -1, tc)
         + w[2:3] * _shifted(x, 0, tc) + w[3:4] * _shifted(x, 1, tc))
    o_ref[...] = y + b_ref[...]


def conv_b(proj, conv_w, bias, seqs, col0, d_rnn):
    m = proj.shape[0]
    tc = seqs.chunk
    cb = _pick(math.gcd(col0, d_rnn), 512, LANES)
    nb = d_rnn // cb
    specs = _halo_specs(seqs, cb, col0 // cb)
    specs.append(pl.BlockSpec((conv_w.shape[0], cb), lambda c, j: (0, j)))
    specs.append(pl.BlockSpec((1, cb), lambda c, j: (0, j)))
    return pl.pallas_call(
        functools.partial(_conv_b_kernel, seqs=seqs),
        out_shape=jax.ShapeDtypeStruct((m, d_rnn), F32),
        grid=(seqs.nc, nb),
        in_specs=specs,
        out_specs=pl.BlockSpec((tc, cb), lambda c, j: (c, j)),
        compiler_params=_params(("parallel", "parallel")),
        name="conv_b",
    )(proj, proj, proj, conv_w, bias.reshape(1, d_rnn))


def _gates_kernel(x_ref, w_ref, ba_ref, bx_ref, lam_ref, a_ref, u_ref, *, hd):
    x = x_ref[...]
    half_x = 0.5 * x
    z = jnp.dot(x.astype(BF16), w_ref[0] * 0.5, preferred_element_type=F32)
    for d in range(2):
        t_r = jnp.tanh(z[:, (2 * d) * hd:(2 * d + 1) * hd] + 0.5 * ba_ref[d:d + 1, :])
        t_i = jnp.tanh(z[:, (2 * d + 1) * hd:(2 * d + 2) * hd] + 0.5 * bx_ref[d:d + 1, :])
        c = (0.5 * LRU_C) * jax.nn.softplus(-lam_ref[d:d + 1, :])
        neg_log_a = c * t_r + c
        a = jnp.exp(-neg_log_a)
        a_ref[d] = a
        u_ref[d] = jnp.sqrt(jnp.tanh(neg_log_a) * (1.0 + a * a)) * ((t_i + 1.0) * half_x)


def lru_gates(xc, w_cat, gate_a_b, gate_x_b, lam):
    m, d_rnn = xc.shape
    heads, hd, _ = w_cat.shape
    tm = _pick(m, 1024, SUBLANES)
    out = jax.ShapeDtypeStruct((2, m, d_rnn), F32)
    vec = pl.BlockSpec((2, hd), lambda i, h: (0, h))
    blk = pl.BlockSpec((2, tm, hd), lambda i, h: (0, i, h))
    return pl.pallas_call(
        functools.partial(_gates_kernel, hd=hd),
        out_shape=(out, out),
        grid=(m // tm, heads),
        in_specs=[pl.BlockSpec((tm, hd), lambda i, h: (i, h)),
                  pl.BlockSpec((1, hd, 4 * hd), lambda i, h: (h, 0, 0)),
                  vec, vec, vec],
        out_specs=(blk, blk),
        compiler_params=_params(("parallel", "parallel")),
        name="lru_gates",
    )(xc, w_cat, gate_a_b, gate_x_b, lam)


def _scan_kernel(a_ref, u_ref, o_ref, h_ref, *, seqs):
    d = pl.program_id(0)
    c = pl.program_id(2)
    cc = jnp.where(d == 0, c, seqs.nc - 1 - c)
    start = jnp.where(d == 0, seqs.is_first(cc), seqs.is_last(cc))
    tc = seqs.chunk

    @pl.when(start)
    def _():
        h_ref[...] = jnp.zeros_like(h_ref)

    def group(g, h):
        for s in range(SUBLANES):
            t = g * SUBLANES + s
            row = jnp.where(d == 0, t, tc - 1 - t)
            h = a_ref[pl.ds(row, 1), :] * h + u_ref[pl.ds(row, 1), :]
            o_ref[pl.ds(row, 1), :] = h
        return h

    h_ref[...] = lax.fori_loop(0, tc // SUBLANES, group, h_ref[...])


def lru_scan(a, u, seqs):
    _, m, d_rnn = a.shape
    tc = seqs.chunk
    cb = _pick(d_rnn, 1024, LANES)

    def idx(d, j, c):
        return (d, jnp.where(d == 0, c, seqs.nc - 1 - c), j)

    blk = pl.BlockSpec((None, tc, cb), idx)
    return pl.pallas_call(
        functools.partial(_scan_kernel, seqs=seqs),
        out_shape=jax.ShapeDtypeStruct((2, m, d_rnn), F32),
        grid=(2, d_rnn // cb, seqs.nc),
        in_specs=[blk, blk],
        out_specs=blk,
        scratch_shapes=[pltpu.VMEM((1, cb), F32)],
        compiler_params=_params(("arbitrary", "arbitrary", "arbitrary")),
        name="lru_scan",
    )(a, u)


def _gate_out_kernel(g_ref, h_ref, o_ref):
    o_ref[...] = (jax.nn.gelu(g_ref[...]) * (h_ref[0] + h_ref[1])).astype(o_ref.dtype)


def gate_out(proj, hs, col0):
    _, m, d_rnn = hs.shape
    tm = _pick(m, 512, SUBLANES)
    cb = _pick(math.gcd(col0, d_rnn), 1024, LANES)
    return pl.pallas_call(
        _gate_out_kernel,
        out_shape=jax.ShapeDtypeStruct((m, d_rnn), BF16),
        grid=(m // tm, d_rnn // cb),
        in_specs=[pl.BlockSpec((tm, cb), lambda i, j: (i, col0 // cb + j)),
                  pl.BlockSpec((2, tm, cb), lambda i, j: (0, i, j))],
        out_specs=pl.BlockSpec((tm, cb), lambda i, j: (i, j)),
        compiler_params=_params(("parallel", "parallel")),
        name="gate_out",
    )(proj, hs)


def _attn_kernel(sink_ref, q_ref, kp_ref, kc_ref, kn_ref, vp_ref, vc_ref, vn_ref, o_ref, *,
                 seqs, n_kv, group, hd, slopes):
    c = pl.program_id(0)
    k_lo = jnp.where(seqs.is_first(c), 0, -BLOCK)
    k_hi = jnp.where(seqs.is_last(c), BLOCK, 2 * BLOCK)
    rows = group * BLOCK
    row = lax.broadcasted_iota(jnp.int32, (rows, 3 * BLOCK), 0)
    k_rel = lax.broadcasted_iota(jnp.int32, (rows, 3 * BLOCK), 1) - BLOCK
    dist = jnp.abs((row & (BLOCK - 1)) - k_rel)
    valid = jnp.logical_and(dist <= WINDOW, jnp.logical_and(k_rel >= k_lo, k_rel < k_hi))
    dist_f = dist.astype(F32)
    g_of_row = lax.shift_right_logical(lax.broadcasted_iota(jnp.int32, (rows, 1), 0),
                                       BLOCK.bit_length() - 1)
    scale = hd ** -0.5

    def per_row(values):
        col = jnp.full((rows, 1), values[group - 1], F32)
        for g in range(group - 2, -1, -1):
            col = jnp.where(g_of_row == g, values[g], col)
        return col

    for kv in range(n_kv):
        ks = slice(kv * hd, (kv + 1) * hd)
        heads = [kv * group + g for g in range(group)]
        k3 = jnp.concatenate([kp_ref[:, ks], kc_ref[:, ks], kn_ref[:, ks]], axis=0)
        v3 = jnp.concatenate([vp_ref[:, ks], vc_ref[:, ks], vn_ref[:, ks]], axis=0)
        q = jnp.concatenate([q_ref[:, h * hd:(h + 1) * hd] for h in heads], axis=0)
        slope = per_row([slopes[h] for h in heads])
        sink = per_row([sink_ref[h] for h in heads])
        s = lax.dot_general(q, k3, (((1,), (1,)), ((), ())), preferred_element_type=F32) * scale
        s = jnp.where(valid, s - slope * dist_f, NEG_INF)
        m = jnp.maximum(jnp.max(s, axis=-1, keepdims=True), sink)
        p = jnp.exp(s - m)
        denom = jnp.sum(p, axis=-1, keepdims=True) + jnp.exp(sink - m)
        probs = (p * (1.0 / denom)).astype(BF16)
        out = jnp.dot(probs, v3, preferred_element_type=F32).astype(o_ref.dtype)
        for g, h in enumerate(heads):
            o_ref[:, h * hd:(h + 1) * hd] = out[g * BLOCK:(g + 1) * BLOCK]


def window_attention(qkv, sink, seqs, n_q, n_kv, hd):
    m = qkv.shape[0]
    assert seqs.chunk == BLOCK
    group = n_q // n_kv
    dq = n_q * hd
    dkv = n_kv * hd
    assert dq % dkv == 0
    kcol = dq // dkv
    vcol = kcol + 1
    last = seqs.nc - 1
    slopes = tuple(2.0 ** (-8.0 * (h + 1.0) / n_q) for h in range(n_q))

    def kv_specs(col):
        return [pl.BlockSpec((BLOCK, dkv), lambda c: (jnp.maximum(c - 1, 0), col)),
                pl.BlockSpec((BLOCK, dkv), lambda c: (c, col)),
                pl.BlockSpec((BLOCK, dkv), lambda c: (jnp.minimum(c + 1, last), col))]

    return pl.pallas_call(
        functools.partial(_attn_kernel, seqs=seqs, n_kv=n_kv, group=group, hd=hd, slopes=slopes),
        out_shape=jax.ShapeDtypeStruct((m, dq), BF16),
        grid=(seqs.nc,),
        in_specs=[pl.BlockSpec(memory_space=pltpu.SMEM),
                  pl.BlockSpec((BLOCK, dq), lambda c: (c, 0))] + kv_specs(kcol) + kv_specs(vcol),
        out_specs=pl.BlockSpec((BLOCK, dq), lambda c: (c, 0)),
        compiler_params=_params(("parallel",)),
        name="window_attention",
    )(sink, qkv, qkv, qkv, qkv, qkv, qkv, qkv)


def _router_kernel(x_ref, g_ref, wr_ref, hn_ref, aff_ref):
    x = x_ref[...]
    ms = jnp.mean(x * x, axis=-1, keepdims=True)
    hn = (x * lax.rsqrt(ms + EPS) * g_ref[...]).astype(BF16)
    half = hn.shape[1] // 2
    bits = pltpu.bitcast(hn.astype(F32), jnp.uint32)
    hn_ref[...] = (lax.shift_right_logical(bits[:, :half], jnp.uint32(16))
                   | (bits[:, half:] & jnp.uint32(0xFFFF0000)))
    logits = lax.dot_general(wr_ref[...], hn, (((1,), (1,)), ((), ())),
                             preferred_element_type=F32)
    mx = jnp.max(logits, axis=0, keepdims=True)
    p = jnp.exp(logits - mx)
    aff_ref[...] = p / jnp.sum(p, axis=0, keepdims=True)


def router(x, g, w_router_t):
    m, d = x.shape
    e = w_router_t.shape[0]
    tm = _pick(m, 512, LANES)
    return pl.pallas_call(
        _router_kernel,
        out_shape=(jax.ShapeDtypeStruct((m, d // 2), jnp.uint32),
                   jax.ShapeDtypeStruct((e, m), F32)),
        grid=(m // tm,),
        in_specs=[pl.BlockSpec((tm, d), lambda i: (i, 0)),
                  pl.BlockSpec((1, d), lambda i: (0, 0)),
                  pl.BlockSpec((e, d), lambda i: (0, 0))],
        out_specs=(pl.BlockSpec((tm, d // 2), lambda i: (i, 0)),
                   pl.BlockSpec((e, tm), lambda i: (0, i))),
        compiler_params=_params(("parallel",)),
        name="router",
    )(x, g.reshape(1, d), w_router_t)


def _total(x):
    return jnp.sum(jnp.sum(x, axis=0, keepdims=True), axis=1, keepdims=True)


def _split3(x):
    x1 = x.astype(BF16)
    r1 = x - x1.astype(F32)
    x2 = r1.astype(BF16)
    x3 = (r1 - x2.astype(F32)).astype(BF16)
    return x1, x2, x3


def _select_kernel(aff_ref, idx_ref, gate_ref, *, cap, n_tok):
    grp = pl.program_id(1)
    a = aff_ref[...]
    rows = a.shape[0]
    bits = pltpu.bitcast(a, jnp.int32)
    thr = jnp.zeros((1, 1), jnp.int32)
    for b in range(30, -1, -1):
        cand = thr | (1 << b)
        cnt = _total((bits >= cand).astype(jnp.int32))
        thr = jnp.where(cnt >= cap, cand, thr)
    gt = bits > thr
    eq = bits == thr
    need = (cap - _total(gt.astype(jnp.int32))).astype(F32)

    def iota(shape, dim):
        return lax.broadcasted_iota(jnp.int32, shape, dim)

    def tri(n, keep):
        return keep(iota((n, n), 0), iota((n, n), 1)).astype(BF16)

    def dot(x, y):
        return jnp.dot(x, y, preferred_element_type=F32)

    def dot_nt(x, y):
        return lax.dot_general(x, y, (((1,), (1,)), ((), ())), preferred_element_type=F32)

    upper = tri(LANES, lambda i, j: i <= j)
    lower = tri(LANES, lambda i, j: j <= i)
    before = tri(rows, lambda i, j: j < i)
    upto = tri(rows, lambda i, j: j <= i)
    eye = tri(LANES, lambda i, j: i == j)

    eq_b = eq.astype(BF16)
    eq_rank = dot(eq_b, upper) + jnp.sum(dot(before, eq_b), axis=1, keepdims=True)
    sel = jnp.logical_or(gt, jnp.logical_and(eq, eq_rank <= need))
    s = sel.astype(BF16)

    s_t = dot_nt(eye, s).astype(BF16)
    p_t = dot(lower, s_t).astype(BF16)
    row_tot = jnp.sum(sel.astype(F32), axis=1, keepdims=True)
    c_in = jnp.sum(dot(upto, s), axis=1, keepdims=True)
    c_ex = c_in - row_tot
    j = iota((1, cap), 1).astype(F32)
    r_j = jnp.sum((c_in <= j).astype(F32), axis=0, keepdims=True)
    g_t = (iota((rows, cap), 0).astype(F32) == r_j)
    g_b = g_t.astype(BF16)
    p_row = dot(p_t, g_b)
    k = j - jnp.sum(jnp.where(g_t, c_ex, 0.0), axis=0, keepdims=True)
    c_j = jnp.sum((p_row <= k).astype(F32), axis=0, keepdims=True)
    idx_ref[...] = (r_j * LANES + c_j).astype(jnp.int32) + grp * n_tok

    a_row = jnp.zeros((LANES, cap), F32)
    for part in _split3(a):
        a_row = a_row + dot(dot_nt(eye, part).astype(BF16), g_b)
    hit = iota((LANES, cap), 0).astype(F32) == c_j
    gate = jnp.sum(jnp.where(hit, a_row, 0.0), axis=0, keepdims=True)
    gate_ref[...] = jnp.transpose(jnp.broadcast_to(gate, (LANES, cap)))


def select_tokens(aff, n_groups, cap):
    e, m = aff.shape
    n_tok = m // n_groups
    rows = n_tok // LANES
    aff4 = aff.reshape(e, n_groups, rows, LANES)
    idx, gate = pl.pallas_call(
        functools.partial(_select_kernel, cap=cap, n_tok=n_tok),
        out_shape=(jax.ShapeDtypeStruct((e * n_groups, 1, cap), jnp.int32),
                   jax.ShapeDtypeStruct((e * n_groups * cap, LANES), F32)),
        grid=(e, n_groups),
        in_specs=[pl.BlockSpec((None, None, rows, LANES), lambda i, g: (i, g, 0, 0))],
        out_specs=(pl.BlockSpec((None, 1, cap), lambda i, g: (i * n_groups + g, 0, 0)),
                   pl.BlockSpec((cap, LANES), lambda i, g: (i * n_groups + g, 0))),
        compiler_params=_params(("parallel", "parallel")),
        name="select_tokens",
    )(aff4)
    return idx, gate


class _RowDma:
    def __init__(self, idx_hbm, idx_smem, buf, sems, tm):
        self.idx_hbm, self.idx_smem, self.buf, self.sems, self.tm = idx_hbm, idx_smem, buf, sems, tm
        self.sub = idx_smem.shape[2]

    def fetch_idx(self, tile, slot):
        cp = pltpu.make_async_copy(self.idx_hbm.at[tile], self.idx_smem.at[slot], self.sems.at[4])
        cp.start()
        cp.wait()

    def _rows(self, slot, hbm, to_hbm):
        for j in range(self.tm):
            tok = self.idx_smem[slot, j // self.sub, j % self.sub]
            row_hbm = hbm.at[pl.ds(tok, 1)]
            row_vmem = self.buf.at[slot, pl.ds(j, 1)]
            if to_hbm:
                pltpu.make_async_copy(row_vmem, row_hbm, self.sems.at[2 + slot]).start()
            else:
                pltpu.make_async_copy(row_hbm, row_vmem, self.sems.at[slot]).start()

    def gather(self, slot, hbm):
        self._rows(slot, hbm, to_hbm=False)

    def scatter(self, slot, hbm):
        self._rows(slot, hbm, to_hbm=True)

    def wait_gather(self, slot, hbm):
        pltpu.make_async_copy(hbm.at[pl.ds(0, self.tm)], self.buf.at[slot], self.sems.at[slot]).wait()

    def wait_scatter(self, slot, hbm):
        pltpu.make_async_copy(self.buf.at[slot], hbm.at[pl.ds(0, self.tm)],
                              self.sems.at[2 + slot]).wait()


def _row_dma_scratch(tm, width, dtype):
    return [pltpu.SMEM((2, SUBLANES, tm // SUBLANES), jnp.int32),
            pltpu.VMEM((2, tm, width), dtype),
            pltpu.SemaphoreType.DMA((5,))]


def _gather_kernel(idx_hbm, hn_hbm, o_ref, idx_smem, buf, sems, *, tm, n_steps):
    s = pl.program_id(0)
    dma = _RowDma(idx_hbm, idx_smem, buf, sems, tm)
    half = buf.shape[2]

    def unpack(slot):
        words = buf[slot]
        rows = slice(slot * tm, (slot + 1) * tm)
        lo = pltpu.bitcast(lax.shift_left(words, jnp.uint32(16)), F32)
        hi = pltpu.bitcast(words & jnp.uint32(0xFFFF0000), F32)
        o_ref[rows, :half] = lo.astype(o_ref.dtype)
        o_ref[rows, half:] = hi.astype(o_ref.dtype)

    @pl.when(s == 0)
    def _():
        dma.fetch_idx(0, 0)
        dma.gather(0, hn_hbm)

    dma.fetch_idx(2 * s + 1, 1)
    dma.gather(1, hn_hbm)
    dma.wait_gather(0, hn_hbm)
    unpack(0)

    @pl.when(s + 1 < n_steps)
    def _():
        dma.fetch_idx(2 * s + 2, 0)
        dma.gather(0, hn_hbm)

    dma.wait_gather(1, hn_hbm)
    unpack(1)


def moe_gather(hn_packed, idx_t, tm):
    half = hn_packed.shape[1]
    d = 2 * half
    n_steps = idx_t.shape[0] // 2
    return pl.pallas_call(
        functools.partial(_gather_kernel, tm=tm, n_steps=n_steps),
        out_shape=jax.ShapeDtypeStruct((2 * n_steps * tm, d), BF16),
        grid=(n_steps,),
        in_specs=[pl.BlockSpec(memory_space=pl.ANY), pl.BlockSpec(memory_space=pl.ANY)],
        out_specs=pl.BlockSpec((2 * tm, d), lambda s: (s, 0)),
        scratch_shapes=_row_dma_scratch(tm, half, jnp.uint32),
        compiler_params=pltpu.CompilerParams(dimension_semantics=("arbitrary",),
                                             vmem_limit_bytes=VMEM_LIMIT_BYTES),
        name="moe_gather",
    )(idx_t, hn_packed)


def _expert_grid(rows_per_expert, n_experts, n_cols, bm, bn):
    per = rows_per_expert // bm
    grid = (n_experts, n_cols // bn, per)

    def rows(e, j, i):
        return e * per + i

    return grid, rows


def _up_kernel(x_ref, wg_ref, wu_ref, o_ref, wg_bf, wu_bf):
    @pl.when(pl.program_id(2) == 0)
    def _():
        wg_bf[...] = wg_ref[...].astype(BF16)
        wu_bf[...] = wu_ref[...].astype(BF16)

    x = x_ref[...]
    hg = jnp.dot(x, wg_bf[...], preferred_element_type=F32)
    hu = jnp.dot(x, wu_bf[...], preferred_element_type=F32)
    o_ref[...] = (jax.nn.silu(hg) * hu).astype(o_ref.dtype)


def moe_up(xe, w_gate, w_up, layer, rows_per_expert):
    m, d = xe.shape
    _, n_e, _, d_f = w_gate.shape
    bm = _pick(rows_per_expert, 1024, 2 * SUBLANES)
    bn = _pick(d_f, 256, LANES)
    grid, rows = _expert_grid(rows_per_expert, n_e, d_f, bm, bn)
    w_spec = pl.BlockSpec((None, None, d, bn), lambda e, j, i: (layer, e, 0, j))
    return pl.pallas_call(
        _up_kernel,
        out_shape=jax.ShapeDtypeStruct((m, d_f), BF16),
        grid=grid,
        in_specs=[pl.BlockSpec((bm, d), lambda e, j, i: (rows(e, j, i), 0)), w_spec, w_spec],
        out_specs=pl.BlockSpec((bm, bn), lambda e, j, i: (rows(e, j, i), j)),
        scratch_shapes=[pltpu.VMEM((d, bn), BF16), pltpu.VMEM((d, bn), BF16)],
        compiler_params=_params(("parallel", "parallel", "arbitrary")),
        name="moe_up",
    )(xe, w_gate, w_up)


def _down_kernel(h_ref, wd_ref, gate_ref, o_ref, wd_bf):
    @pl.when(pl.program_id(2) == 0)
    def _():
        wd_bf[...] = wd_ref[...].astype(BF16)

    y = jnp.dot(h_ref[...], wd_bf[...], preferred_element_type=F32)
    o_ref[...] = (y * gate_ref[:, 0:1]).astype(o_ref.dtype)


def moe_down(h, w_down, gate, layer, rows_per_expert):
    m, d_f = h.shape
    _, n_e, _, d = w_down.shape
    bm = _pick(rows_per_expert, 1024, 2 * SUBLANES)
    bn = _pick(d, 1024, LANES)
    grid, rows = _expert_grid(rows_per_expert, n_e, d, bm, bn)
    return pl.pallas_call(
        _down_kernel,
        out_shape=jax.ShapeDtypeStruct((m, d), BF16),
        grid=grid,
        in_specs=[pl.BlockSpec((bm, d_f), lambda e, j, i: (rows(e, j, i), 0)),
                  pl.BlockSpec((None, None, d_f, bn), lambda e, j, i: (layer, e, 0, j)),
                  pl.BlockSpec((bm, LANES), lambda e, j, i: (rows(e, j, i), 0))],
        out_specs=pl.BlockSpec((bm, bn), lambda e, j, i: (rows(e, j, i), j)),
        scratch_shapes=[pltpu.VMEM((d_f, bn), BF16)],
        compiler_params=_params(("parallel", "parallel", "arbitrary")),
        name="moe_down",
    )(h, w_down, gate)


def _combine_kernel(idx_hbm, ye_ref, x_hbm, o_hbm, idx_smem, buf, sems, *, tm, n_steps):
    del x_hbm
    s = pl.program_id(0)
    dma = _RowDma(idx_hbm, idx_smem, buf, sems, tm)

    @pl.when(s == 0)
    def _():
        dma.fetch_idx(0, 0)
        dma.gather(0, o_hbm)

    @pl.when(s > 0)
    def _():
        dma.wait_scatter(1, o_hbm)

    dma.fetch_idx(2 * s + 1, 1)
    dma.gather(1, o_hbm)
    for slot in range(2):
        dma.wait_gather(slot, o_hbm)
        buf[slot] = buf[slot] + ye_ref[slot * tm:(slot + 1) * tm].astype(F32)
        dma.scatter(slot, o_hbm)
    dma.wait_scatter(0, o_hbm)

    @pl.when(s + 1 < n_steps)
    def _():
        dma.fetch_idx(2 * s + 2, 0)
        dma.gather(0, o_hbm)

    @pl.when(s == n_steps - 1)
    def _():
        dma.wait_scatter(1, o_hbm)


def moe_combine(x, ye, idx_t, tm):
    m, d = x.shape
    n_steps = idx_t.shape[0] // 2
    return pl.pallas_call(
        functools.partial(_combine_kernel, tm=tm, n_steps=n_steps),
        out_shape=jax.ShapeDtypeStruct((m, d), F32),
        grid=(n_steps,),
        in_specs=[pl.BlockSpec(memory_space=pl.ANY),
                  pl.BlockSpec((2 * tm, d), lambda s: (s, 0)),
                  pl.BlockSpec(memory_space=pl.ANY)],
        out_specs=pl.BlockSpec(memory_space=pl.ANY),
        scratch_shapes=_row_dma_scratch(tm, d, F32),
        input_output_aliases={2: 0},
        compiler_params=pltpu.CompilerParams(dimension_semantics=("arbitrary",),
                                             vmem_limit_bytes=VMEM_LIMIT_BYTES),
        name="moe_combine",
    )(idx_t, ye, x)


def expert_choice_moe(x, norm_g, w_router, w_gate, w_up, w_down, layer, n_groups):
    m = x.shape[0]
    e = w_router.shape[1]
    cap = max(1, CAPACITY_FACTOR * (m // n_groups) // e)
    hn_packed, aff = router(x, norm_g, jnp.transpose(w_router).astype(BF16))
    idx, gate = select_tokens(aff, n_groups, cap)
    tm = _pick(cap, 512, SUBLANES * SUBLANES)
    assert (e * n_groups * cap) % (2 * tm) == 0
    idx_t = idx.reshape(-1, SUBLANES, tm // SUBLANES)
    xe = moe_gather(hn_packed, idx_t, tm)
    h = moe_up(xe, w_gate, w_up, layer, n_groups * cap)
    ye = moe_down(h, w_down, gate, layer, n_groups * cap)
    return moe_combine(x, ye, idx_t, tm)


def kernel(x_prompt, x_sample, mix_norm, ffn_norm, final_norm, ab_w_in, ab_conv_a, ab_conv_b,
           ab_conv_b_bias, ab_gate_a_w, ab_gate_a_b, ab_gate_x_w, ab_gate_x_b, ab_lambda, ab_w_out,
           attn_w_qkv, attn_w_o, attn_sink, moe_w_router, moe_w_gate, moe_w_up, moe_w_down):
    n0, t0, d = x_prompt.shape
    n1, t1, _ = x_sample.shape
    assert n0 * t0 == n1 * t1, "expert capacity is per group: groups must hold equally many tokens"
    depth = mix_norm.shape[0]
    d_conv = ab_conv_a.shape[-1]
    d_rnn = ab_conv_b.shape[-1]
    n_q = attn_sink.shape[-1]
    hd = attn_w_o.shape[1] // n_q
    n_kv = (attn_w_qkv.shape[-1] // hd - n_q) // 2
    seqs_conv = _Seqs(n0, t0, n1, t1, _pick(math.gcd(t0, t1), 512, SUBLANES))
    seqs_attn = _Seqs(n0, t0, n1, t1, BLOCK)

    x = jnp.concatenate([x_prompt.reshape(n0 * t0, d), x_sample.reshape(n1 * t1, d)], axis=0)
    for layer in range(depth):
        j = layer // 2
        hn = rmsnorm(x, mix_norm[layer], BF16)
        if layer % 2 == 0:
            proj = matmul([hn], [ab_w_in[j].astype(BF16)])
            y_a = conv_a(proj, ab_conv_a[j], seqs_conv, d_conv)
            xc = conv_b(proj, ab_conv_b[j], ab_conv_b_bias[j], seqs_conv, 3 * d_conv + d_rnn, d_rnn)
            w_cat = jnp.concatenate([ab_gate_a_w[j, 0], ab_gate_x_w[j, 0],
                                     ab_gate_a_w[j, 1], ab_gate_x_w[j, 1]], axis=-1).astype(BF16)
            a, u = lru_gates(xc, w_cat, ab_gate_a_b[j], ab_gate_x_b[j], ab_lambda[j])
            hs = lru_scan(a, u, seqs_conv)
            y_b = gate_out(proj, hs, 3 * d_conv)
            w_out = ab_w_out[j].astype(BF16)
            x = matmul([y_a, y_b], [w_out[:d_conv], w_out[d_conv:]], res=x, bn=512)
        else:
            qkv = matmul([hn], [attn_w_qkv[j].astype(BF16)], out_dtype=BF16)
            att = window_attention(qkv, attn_sink[j], seqs_attn, n_q, n_kv, hd)
            x = matmul([att], [attn_w_o[j].astype(BF16)], res=x, bn=512)
        x = expert_choice_moe(x, ffn_norm[layer], moe_w_router[layer], moe_w_gate, moe_w_up,
                              moe_w_down, layer, n_groups=2)
    y_prompt = rmsnorm(x, final_norm, F32, row0=0, rows=n0 * t0)
    y_sample = rmsnorm(x, final_norm, F32, row0=n0 * t0, rows=n1 * t1)
    return (y_prompt.reshape(n0, t0, d), y_sample.reshape(n1, t1, d))
```

```python
import functools
import math

import jax
import jax.numpy as jnp
from jax import lax
from jax.experimental import pallas as pl
from jax.experimental.pallas import tpu as pltpu

EPS = 1e-6
LRU_C = 8.0
WINDOW = 128
BLOCK = 128
NEG_INF = -1e30
CAPACITY_FACTOR = 2

LANES = 128
SUBLANES = 8
VMEM_LIMIT_BYTES = 56 * 1024 * 1024

F32 = jnp.float32
BF16 = jnp.bfloat16


def _pick(n, target, quantum):
    if n <= target:
        return n
    t = (target // quantum) * quantum
    while t >= quantum:
        if n % t == 0:
            return t
        t -= quantum
    return n


def _params(sem, vmem=VMEM_LIMIT_BYTES):
    return pltpu.CompilerParams(dimension_semantics=sem, vmem_limit_bytes=vmem)


class _Seqs:
    def __init__(self, n0, t0, n1, t1, chunk):
        assert t0 % chunk == 0 and t1 % chunk == 0
        self.chunk = chunk
        self.cps0 = t0 // chunk
        self.cps1 = t1 // chunk
        self.nc0 = n0 * self.cps0
        self.nc = self.nc0 + n1 * self.cps1

    def is_first(self, c):
        return jnp.where(c < self.nc0, c % self.cps0 == 0, (c - self.nc0) % self.cps1 == 0)

    def is_last(self, c):
        return jnp.where(c < self.nc0, c % self.cps0 == self.cps0 - 1,
                         (c - self.nc0) % self.cps1 == self.cps1 - 1)


def _rmsnorm_kernel(x_ref, g_ref, o_ref):
    x = x_ref[...]
    ms = jnp.mean(x * x, axis=-1, keepdims=True)
    o_ref[...] = (x * lax.rsqrt(ms + EPS) * g_ref[...]).astype(o_ref.dtype)


def rmsnorm(x, g, out_dtype, row0=0, rows=None):
    d = x.shape[1]
    m = x.shape[0] if rows is None else rows
    tm = _pick(math.gcd(m, row0), 512, SUBLANES)
    blk0 = row0 // tm
    return pl.pallas_call(
        _rmsnorm_kernel,
        out_shape=jax.ShapeDtypeStruct((m, d), out_dtype),
        grid=(m // tm,),
        in_specs=[pl.BlockSpec((tm, d), lambda i: (blk0 + i, 0)),
                  pl.BlockSpec((1, d), lambda i: (0, 0))],
        out_specs=pl.BlockSpec((tm, d), lambda i: (i, 0)),
        compiler_params=_params(("parallel",)),
        name="rmsnorm",
    )(x, g.reshape(1, d))


def _matmul_kernel(*refs, n_pairs, has_res):
    xs = refs[:n_pairs]
    ws = refs[n_pairs:2 * n_pairs]
    o_ref = refs[-1]
    acc = jnp.dot(xs[0][...], ws[0][...], preferred_element_type=F32)
    for x_ref, w_ref in zip(xs[1:], ws[1:]):
        acc = acc + jnp.dot(x_ref[...], w_ref[...], preferred_element_type=F32)
    if has_res:
        acc = acc + refs[2 * n_pairs][...]
    o_ref[...] = acc.astype(o_ref.dtype)


def matmul(xs, ws, res=None, out_dtype=F32, bm=1024, bn=1024):
    m = xs[0].shape[0]
    n = ws[0].shape[1]
    bm = _pick(m, bm, SUBLANES)
    bn = _pick(n, bn, LANES)
    in_specs = [pl.BlockSpec((bm, x.shape[1]), lambda i, j: (i, 0)) for x in xs]
    in_specs += [pl.BlockSpec((w.shape[0], bn), lambda i, j: (0, j)) for w in ws]
    args = list(xs) + list(ws)
    if res is not None:
        in_specs.append(pl.BlockSpec((bm, bn), lambda i, j: (i, j)))
        args.append(res)
    return pl.pallas_call(
        functools.partial(_matmul_kernel, n_pairs=len(xs), has_res=res is not None),
        out_shape=jax.ShapeDtypeStruct((m, n), out_dtype),
        grid=(m // bm, n // bn),
        in_specs=in_specs,
        out_specs=pl.BlockSpec((bm, bn), lambda i, j: (i, j)),
        compiler_params=_params(("parallel", "parallel")),
        name="matmul",
    )(*args)


def _halo_specs(seqs, cb, col0):
    tc = seqs.chunk
    r8 = tc // SUBLANES
    last8 = seqs.nc * r8 - 1
    main = pl.BlockSpec((tc, cb), lambda c, j: (c, col0 + j))
    prev = pl.BlockSpec((SUBLANES, cb), lambda c, j: (jnp.maximum(c * r8 - 1, 0), col0 + j))
    nxt = pl.BlockSpec((SUBLANES, cb), lambda c, j: (jnp.minimum((c + 1) * r8, last8), col0 + j))
    return [prev, main, nxt]


def _extended(prev, main, nxt, first, last):
    prev = jnp.where(first, 0.0, prev)
    nxt = jnp.where(last, 0.0, nxt)
    return jnp.concatenate([prev, main, nxt], axis=0)


def _shifted(ext, offset, tc):
    n = ext.shape[0]
    if offset == 0:
        return ext[SUBLANES:SUBLANES + tc]
    return pltpu.roll(ext, (-offset) % n, axis=0)[SUBLANES:SUBLANES + tc]


def _conv_a_kernel(cp_ref, c_ref, cn_ref, vp_ref, v_ref, vn_ref, b_ref, w_ref, o_ref, *, seqs):
    c = pl.program_id(0)
    first, last = seqs.is_first(c), seqs.is_last(c)
    tc = seqs.chunk
    z = _extended(cp_ref[...] * vp_ref[...], c_ref[...] * v_ref[...], cn_ref[...] * vn_ref[...],
                  first, last)
    w = w_ref[...]
    y = (w[0:1] * _shifted(z, -1, tc) + w[1:2] * _shifted(z, 0, tc) + w[2:3] * _shifted(z, 1, tc))
    o_ref[...] = (b_ref[...] * y).astype(o_ref.dtype)


def conv_a(proj, conv_w, seqs, d_conv):
    m = proj.shape[0]
    tc = seqs.chunk
    cb = _pick(d_conv, 512, LANES)
    nb = d_conv // cb
    specs = _halo_specs(seqs, cb, nb) + _halo_specs(seqs, cb, 2 * nb)
    specs.append(pl.BlockSpec((tc, cb), lambda c, j: (c, j)))
    specs.append(pl.BlockSpec((conv_w.shape[0], cb), lambda c, j: (0, j)))
    return pl.pallas_call(
        functools.partial(_conv_a_kernel, seqs=seqs),
        out_shape=jax.ShapeDtypeStruct((m, d_conv), BF16),
        grid=(seqs.nc, nb),
        in_specs=specs,
        out_specs=pl.BlockSpec((tc, cb), lambda c, j: (c, j)),
        compiler_params=_params(("parallel", "parallel")),
        name="conv_a",
    )(proj, proj, proj, proj, proj, proj, proj, conv_w)


def _conv_b_kernel(xp_ref, x_ref, xn_ref, w_ref, b_ref, o_ref, *, seqs):
    c = pl.program_id(0)
    tc = seqs.chunk
    x = _extended(xp_ref[...], x_ref[...], xn_ref[...], seqs.is_first(c), seqs.is_last(c))
    w = w_ref[...]
    y = (w[0:1] * _shifted(x, -2, tc) + w[1:2] * _shifted(x, -1, tc)
         + w[2:3] * _shifted(x, 0, tc) + w[3:4] * _shifted(x, 1, tc))
    o_ref[...] = y + b_ref[...]


def conv_b(proj, conv_w, bias, seqs, col0, d_rnn):
    m = proj.shape[0]
    tc = seqs.chunk
    cb = _pick(math.gcd(col0, d_rnn), 512, LANES)
    nb = d_rnn // cb
    specs = _halo_specs(seqs, cb, col0 // cb)
    specs.append(pl.BlockSpec((conv_w.shape[0], cb), lambda c, j: (0, j)))
    specs.append(pl.BlockSpec((1, cb), lambda c, j: (0, j)))
    return pl.pallas_call(
        functools.partial(_conv_b_kernel, seqs=seqs),
        out_shape=jax.ShapeDtypeStruct((m, d_rnn), F32),
        grid=(seqs.nc, nb),
        in_specs=specs,
        out_specs=pl.BlockSpec((tc, cb), lambda c, j: (c, j)),
        compiler_params=_params(("parallel", "parallel")),
        name="conv_b",
    )(proj, proj, proj, conv_w, bias.reshape(1, d_rnn))


def _gates_kernel(x_ref, w_ref, ba_ref, bx_ref, lam_ref, a_ref, u_ref, *, hd):
    x = x_ref[...]
    half_x = 0.5 * x
    z = jnp.dot(x.astype(BF16), w_ref[0] * 0.5, preferred_element_type=F32)
    for d in range(2):
        t_r = jnp.tanh(z[:, (2 * d) * hd:(2 * d + 1) * hd] + 0.5 * ba_ref[d:d + 1, :])
        t_i = jnp.tanh(z[:, (2 * d + 1) * hd:(2 * d + 2) * hd] + 0.5 * bx_ref[d:d + 1, :])
        c = (0.5 * LRU_C) * jax.nn.softplus(-lam_ref[d:d + 1, :])
        neg_log_a = c * t_r + c
        a = jnp.exp(-neg_log_a)
        a_ref[d] = a
        u_ref[d] = jnp.sqrt(jnp.tanh(neg_log_a) * (1.0 + a * a)) * ((t_i + 1.0) * half_x)


def lru_gates(xc, w_cat, gate_a_b, gate_x_b, lam):
    m, d_rnn = xc.shape
    heads, hd, _ = w_cat.shape
    tm = _pick(m, 1024, SUBLANES)
    out = jax.ShapeDtypeStruct((2, m, d_rnn), F32)
    vec = pl.BlockSpec((2, hd), lambda i, h: (0, h))
    blk = pl.BlockSpec((2, tm, hd), lambda i, h: (0, i, h))
    return pl.pallas_call(
        functools.partial(_gates_kernel, hd=hd),
        out_shape=(out, out),
        grid=(m // tm, heads),
        in_specs=[pl.BlockSpec((tm, hd), lambda i, h: (i, h)),
                  pl.BlockSpec((1, hd, 4 * hd), lambda i, h: (h, 0, 0)),
                  vec, vec, vec],
        out_specs=(blk, blk),
        compiler_params=_params(("parallel", "parallel")),
        name="lru_gates",
    )(xc, w_cat, gate_a_b, gate_x_b, lam)


def _scan_kernel(a_ref, u_ref, o_ref, h_ref, *, seqs):
    d = pl.program_id(0)
    c = pl.program_id(2)
    cc = jnp.where(d == 0, c, seqs.nc - 1 - c)
    start = jnp.where(d == 0, seqs.is_first(cc), seqs.is_last(cc))
    tc = seqs.chunk

    @pl.when(start)
    def _():
        h_ref[...] = jnp.zeros_like(h_ref)

    def group(g, h):
        for s in range(SUBLANES):
            t = g * SUBLANES + s
            row = jnp.where(d == 0, t, tc - 1 - t)
            h = a_ref[pl.ds(row, 1), :] * h + u_ref[pl.ds(row, 1), :]
            o_ref[pl.ds(row, 1), :] = h
        return h

    h_ref[...] = lax.fori_loop(0, tc // SUBLANES, group, h_ref[...])


def lru_scan(a, u, seqs):
    _, m, d_rnn = a.shape
    tc = seqs.chunk
    cb = _pick(d_rnn, 1024, LANES)

    def idx(d, j, c):
        return (d, jnp.where(d == 0, c, seqs.nc - 1 - c), j)

    blk = pl.BlockSpec((None, tc, cb), idx)
    return pl.pallas_call(
        functools.partial(_scan_kernel, seqs=seqs),
        out_shape=jax.ShapeDtypeStruct((2, m, d_rnn), F32),
        grid=(2, d_rnn // cb, seqs.nc),
        in_specs=[blk, blk],
        out_specs=blk,
        scratch_shapes=[pltpu.VMEM((1, cb), F32)],
        compiler_params=_params(("arbitrary", "arbitrary", "arbitrary")),
        name="lru_scan",
    )(a, u)


def _gate_out_kernel(g_ref, h_ref, o_ref):
    o_ref[...] = (jax.nn.gelu(g_ref[...]) * (h_ref[0] + h_ref[1])).astype(o_ref.dtype)


def gate_out(proj, hs, col0):
    _, m, d_rnn = hs.shape
    tm = _pick(m, 512, SUBLANES)
    cb = _pick(math.gcd(col0, d_rnn), 1024, LANES)
    return pl.pallas_call(
        _gate_out_kernel,
        out_shape=jax.ShapeDtypeStruct((m, d_rnn), BF16),
        grid=(m // tm, d_rnn // cb),
        in_specs=[pl.BlockSpec((tm, cb), lambda i, j: (i, col0 // cb + j)),
                  pl.BlockSpec((2, tm, cb), lambda i, j: (0, i, j))],
        out_specs=pl.BlockSpec((tm, cb), lambda i, j: (i, j)),
        compiler_params=_params(("parallel", "parallel")),
        name="gate_out",
    )(proj, hs)


def _attn_kernel(sink_ref, q_ref, kp_ref, kc_ref, kn_ref, vp_ref, vc_ref, vn_ref, o_ref, *,
                 seqs, n_kv, group, hd, slopes):
    c = pl.program_id(0)
    k_lo = jnp.where(seqs.is_first(c), 0, -BLOCK)
    k_hi = jnp.where(seqs.is_last(c), BLOCK, 2 * BLOCK)
    rows = group * BLOCK
    row = lax.broadcasted_iota(jnp.int32, (rows, 3 * BLOCK), 0)
    k_rel = lax.broadcasted_iota(jnp.int32, (rows, 3 * BLOCK), 1) - BLOCK
    dist = jnp.abs((row & (BLOCK - 1)) - k_rel)
    valid = jnp.logical_and(dist <= WINDOW, jnp.logical_and(k_rel >= k_lo, k_rel < k_hi))
    dist_f = dist.astype(F32)
    g_of_row = lax.shift_right_logical(lax.broadcasted_iota(jnp.int32, (rows, 1), 0),
                                       BLOCK.bit_length() - 1)
    scale = hd ** -0.5

    def per_row(values):
        col = jnp.full((rows, 1), values[group - 1], F32)
        for g in range(group - 2, -1, -1):
            col = jnp.where(g_of_row == g, values[g], col)
        return col

    for kv in range(n_kv):
        ks = slice(kv * hd, (kv + 1) * hd)
        heads = [kv * group + g for g in range(group)]
        k3 = jnp.concatenate([kp_ref[:, ks], kc_ref[:, ks], kn_ref[:, ks]], axis=0)
        v3 = jnp.concatenate([vp_ref[:, ks], vc_ref[:, ks], vn_ref[:, ks]], axis=0)
        q = jnp.concatenate([q_ref[:, h * hd:(h + 1) * hd] for h in heads], axis=0)
        slope = per_row([slopes[h] for h in heads])
        sink = per_row([sink_ref[h] for h in heads])
        s = lax.dot_general(q, k3, (((1,), (1,)), ((), ())), preferred_element_type=F32) * scale
        s = jnp.where(valid, s - slope * dist_f, NEG_INF)
        m = jnp.maximum(jnp.max(s, axis=-1, keepdims=True), sink)
        p = jnp.exp(s - m)
        denom = jnp.sum(p, axis=-1, keepdims=True) + jnp.exp(sink - m)
        probs = (p * (1.0 / denom)).astype(BF16)
        out = jnp.dot(probs, v3, preferred_element_type=F32).astype(o_ref.dtype)
        for g, h in enumerate(heads):
            o_ref[:, h * hd:(h + 1) * hd] = out[g * BLOCK:(g + 1) * BLOCK]


def window_attention(qkv, sink, seqs, n_q, n_kv, hd):
    m = qkv.shape[0]
    assert seqs.chunk == BLOCK
    group = n_q // n_kv
    dq = n_q * hd
    dkv = n_kv * hd
    assert dq % dkv == 0
    kcol = dq // dkv
    vcol = kcol + 1
    last = seqs.nc - 1
    slopes = tuple(2.0 ** (-8.0 * (h + 1.0) / n_q) for h in range(n_q))

    def kv_specs(col):
        return [pl.BlockSpec((BLOCK, dkv), lambda c: (jnp.maximum(c - 1, 0), col)),
                pl.BlockSpec((BLOCK, dkv), lambda c: (c, col)),
                pl.BlockSpec((BLOCK, dkv), lambda c: (jnp.minimum(c + 1, last), col))]

    return pl.pallas_call(
        functools.partial(_attn_kernel, seqs=seqs, n_kv=n_kv, group=group, hd=hd, slopes=slopes),
        out_shape=jax.ShapeDtypeStruct((m, dq), BF16),
        grid=(seqs.nc,),
        in_specs=[pl.BlockSpec(memory_space=pltpu.SMEM),
                  pl.BlockSpec((BLOCK, dq), lambda c: (c, 0))] + kv_specs(kcol) + kv_specs(vcol),
        out_specs=pl.BlockSpec((BLOCK, dq), lambda c: (c, 0)),
        compiler_params=_params(("parallel",)),
        name="window_attention",
    )(sink, qkv, qkv, qkv, qkv, qkv, qkv, qkv)


def _router_kernel(x_ref, g_ref, wr_ref, hn_ref, aff_ref):
    x = x_ref[...]
    ms = jnp.mean(x * x, axis=-1, keepdims=True)
    hn = (x * lax.rsqrt(ms + EPS) * g_ref[...]).astype(BF16)
    half = hn.shape[1] // 2
    bits = pltpu.bitcast(hn.astype(F32), jnp.uint32)
    hn_ref[...] = (lax.shift_right_logical(bits[:, :half], jnp.uint32(16))
                   | (bits[:, half:] & jnp.uint32(0xFFFF0000)))
    logits = lax.dot_general(wr_ref[...], hn, (((1,), (1,)), ((), ())),
                             preferred_element_type=F32)
    mx = jnp.max(logits, axis=0, keepdims=True)
    p = jnp.exp(logits - mx)
    aff_ref[...] = p / jnp.sum(p, axis=0, keepdims=True)


def router(x, g, w_router_t):
    m, d = x.shape
    e = w_router_t.shape[0]
    tm = _pick(m, 512, LANES)
    return pl.pallas_call(
        _router_kernel,
        out_shape=(jax.ShapeDtypeStruct((m, d // 2), jnp.uint32),
                   jax.ShapeDtypeStruct((e, m), F32)),
        grid=(m // tm,),
        in_specs=[pl.BlockSpec((tm, d), lambda i: (i, 0)),
                  pl.BlockSpec((1, d), lambda i: (0, 0)),
                  pl.BlockSpec((e, d), lambda i: (0, 0))],
        out_specs=(pl.BlockSpec((tm, d // 2), lambda i: (i, 0)),
                   pl.BlockSpec((e, tm), lambda i: (0, i))),
        compiler_params=_params(("parallel",)),
        name="router",
    )(x, g.reshape(1, d), w_router_t)


def _total(x):
    return jnp.sum(jnp.sum(x, axis=0, keepdims=True), axis=1, keepdims=True)


def _split3(x):
    x1 = x.astype(BF16)
    r1 = x - x1.astype(F32)
    x2 = r1.astype(BF16)
    x3 = (r1 - x2.astype(F32)).astype(BF16)
    return x1, x2, x3


def _select_kernel(aff_ref, idx_ref, gate_ref, *, cap, n_tok):
    grp = pl.program_id(1)
    a = aff_ref[...]
    rows = a.shape[0]
    bits = pltpu.bitcast(a, jnp.int32)
    thr = jnp.zeros((1, 1), jnp.int32)
    for b in range(30, -1, -1):
        cand = thr | (1 << b)
        cnt = _total((bits >= cand).astype(jnp.int32))
        thr = jnp.where(cnt >= cap, cand, thr)
    gt = bits > thr
    eq = bits == thr
    need = (cap - _total(gt.astype(jnp.int32))).astype(F32)

    def iota(shape, dim):
        return lax.broadcasted_iota(jnp.int32, shape, dim)

    def tri(n, keep):
        return keep(iota((n, n), 0), iota((n, n), 1)).astype(BF16)

    def dot(x, y):
        return jnp.dot(x, y, preferred_element_type=F32)

    def dot_nt(x, y):
        return lax.dot_general(x, y, (((1,), (1,)), ((), ())), preferred_element_type=F32)

    upper = tri(LANES, lambda i, j: i <= j)
    lower = tri(LANES, lambda i, j: j <= i)
    before = tri(rows, lambda i, j: j < i)
    upto = tri(rows, lambda i, j: j <= i)
    eye = tri(LANES, lambda i, j: i == j)

    eq_b = eq.astype(BF16)
    eq_rank = dot(eq_b, upper) + jnp.sum(dot(before, eq_b), axis=1, keepdims=True)
    sel = jnp.logical_or(gt, jnp.logical_and(eq, eq_rank <= need))
    s = sel.astype(BF16)

    s_t = dot_nt(eye, s).astype(BF16)
    p_t = dot(lower, s_t).astype(BF16)
    row_tot = jnp.sum(sel.astype(F32), axis=1, keepdims=True)
    c_in = jnp.sum(dot(upto, s), axis=1, keepdims=True)
    c_ex = c_in - row_tot
    j = iota((1, cap), 1).astype(F32)
    r_j = jnp.sum((c_in <= j).astype(F32), axis=0, keepdims=True)
    g_t = (iota((rows, cap), 0).astype(F32) == r_j)
    g_b = g_t.astype(BF16)
    p_row = dot(p_t, g_b)
    k = j - jnp.sum(jnp.where(g_t, c_ex, 0.0), axis=0, keepdims=True)
    c_j = jnp.sum((p_row <= k).astype(F32), axis=0, keepdims=True)
    idx_ref[...] = (r_j * LANES + c_j).astype(jnp.int32) + grp * n_tok

    a_row = jnp.zeros((LANES, cap), F32)
    for part in _split3(a):
        a_row = a_row + dot(dot_nt(eye, part).astype(BF16), g_b)
    hit = iota((LANES, cap), 0).astype(F32) == c_j
    gate = jnp.sum(jnp.where(hit, a_row, 0.0), axis=0, keepdims=True)
    gate_ref[...] = jnp.transpose(jnp.broadcast_to(gate, (LANES, cap)))


def select_tokens(aff, n_groups, cap):
    e, m = aff.shape
    n_tok = m // n_groups
    rows = n_tok // LANES
    aff4 = aff.reshape(e, n_groups, rows, LANES)
    idx, gate = pl.pallas_call(
        functools.partial(_select_kernel, cap=cap, n_tok=n_tok),
        out_shape=(jax.ShapeDtypeStruct((e * n_groups, 1, cap), jnp.int32),
                   jax.ShapeDtypeStruct((e * n_groups * cap, LANES), F32)),
        grid=(e, n_groups),
        in_specs=[pl.BlockSpec((None, None, rows, LANES), lambda i, g: (i, g, 0, 0))],
        out_specs=(pl.BlockSpec((None, 1, cap), lambda i, g: (i * n_groups + g, 0, 0)),
                   pl.BlockSpec((cap, LANES), lambda i, g: (i * n_groups + g, 0))),
        compiler_params=_params(("parallel", "parallel")),
        name="select_tokens",
    )(aff4)
    return idx, gate


class _RowDma:
    def __init__(self, idx_hbm, idx_smem, buf, sems, tm):
        self.idx_hbm, self.idx_smem, self.buf, self.sems, self.tm = idx_hbm, idx_smem, buf, sems, tm
        self.sub = idx_smem.shape[2]

    def fetch_idx(self, tile, slot):
        cp = pltpu.make_async_copy(self.idx_hbm.at[tile], self.idx_smem.at[slot], self.sems.at[4])
        cp.start()
        cp.wait()

    def _rows(self, slot, hbm, to_hbm):
        for j in range(self.tm):
            tok = self.idx_smem[slot, j // self.sub, j % self.sub]
            row_hbm = hbm.at[pl.ds(tok, 1)]
            row_vmem = self.buf.at[slot, pl.ds(j, 1)]
            if to_hbm:
                pltpu.make_async_copy(row_vmem, row_hbm, self.sems.at[2 + slot]).start(priority=j % 2)
            else:
                pltpu.make_async_copy(row_hbm, row_vmem, self.sems.at[slot]).start(priority=j % 2)

    def gather(self, slot, hbm):
        self._rows(slot, hbm, to_hbm=False)

    def scatter(self, slot, hbm):
        self._rows(slot, hbm, to_hbm=True)

    def wait_gather(self, slot, hbm):
        pltpu.make_async_copy(hbm.at[pl.ds(0, self.tm)], self.buf.at[slot], self.sems.at[slot]).wait()

    def wait_scatter(self, slot, hbm):
        pltpu.make_async_copy(self.buf.at[slot], hbm.at[pl.ds(0, self.tm)],
                              self.sems.at[2 + slot]).wait()


def _row_dma_scratch(tm, width, dtype):
    return [pltpu.SMEM((2, SUBLANES, tm // SUBLANES), jnp.int32),
            pltpu.VMEM((2, tm, width), dtype),
            pltpu.SemaphoreType.DMA((5,))]


def _gather_kernel(idx_hbm, hn_hbm, o_ref, idx_smem, buf, sems, *, tm, n_steps):
    s = pl.program_id(0)
    dma = _RowDma(idx_hbm, idx_smem, buf, sems, tm)
    half = buf.shape[2]

    def unpack(slot):
        words = buf[slot]
        rows = slice(slot * tm, (slot + 1) * tm)
        lo = pltpu.bitcast(lax.shift_left(words, jnp.uint32(16)), F32)
        hi = pltpu.bitcast(words & jnp.uint32(0xFFFF0000), F32)
        o_ref[rows, :half] = lo.astype(o_ref.dtype)
        o_ref[rows, half:] = hi.astype(o_ref.dtype)

    @pl.when(s == 0)
    def _():
        dma.fetch_idx(0, 0)
        dma.gather(0, hn_hbm)

    dma.fetch_idx(2 * s + 1, 1)
    dma.gather(1, hn_hbm)
    dma.wait_gather(0, hn_hbm)
    unpack(0)

    @pl.when(s + 1 < n_steps)
    def _():
        dma.fetch_idx(2 * s + 2, 0)
        dma.gather(0, hn_hbm)

    dma.wait_gather(1, hn_hbm)
    unpack(1)


def moe_gather(hn_packed, idx_t, tm):
    half = hn_packed.shape[1]
    d = 2 * half
    n_steps = idx_t.shape[0] // 2
    return pl.pallas_call(
        functools.partial(_gather_kernel, tm=tm, n_steps=n_steps),
        out_shape=jax.ShapeDtypeStruct((2 * n_steps * tm, d), BF16),
        grid=(n_steps,),
        in_specs=[pl.BlockSpec(memory_space=pl.ANY), pl.BlockSpec(memory_space=pl.ANY)],
        out_specs=pl.BlockSpec((2 * tm, d), lambda s: (s, 0)),
        scratch_shapes=_row_dma_scratch(tm, half, jnp.uint32),
        compiler_params=pltpu.CompilerParams(dimension_semantics=("arbitrary",),
                                             vmem_limit_bytes=VMEM_LIMIT_BYTES),
        name="moe_gather",
    )(idx_t, hn_packed)


def _expert_grid(rows_per_expert, n_experts, n_cols, bm, bn):
    per = rows_per_expert // bm
    grid = (n_experts, n_cols // bn, per)

    def rows(e, j, i):
        return e * per + i

    return grid, rows


def _up_kernel(x_ref, wg_ref, wu_ref, o_ref, wg_bf, wu_bf):
    @pl.when(pl.program_id(2) == 0)
    def _():
        wg_bf[...] = wg_ref[...].astype(BF16)
        wu_bf[...] = wu_ref[...].astype(BF16)

    x = x_ref[...]
    hg = jnp.dot(x, wg_bf[...], preferred_element_type=F32)
    hu = jnp.dot(x, wu_bf[...], preferred_element_type=F32)
    o_ref[...] = (jax.nn.silu(hg) * hu).astype(o_ref.dtype)


def moe_up(xe, w_gate, w_up, layer, rows_per_expert):
    m, d = xe.shape
    _, n_e, _, d_f = w_gate.shape
    bm = _pick(rows_per_expert, 1024, 2 * SUBLANES)
    bn = _pick(d_f, 256, LANES)
    grid, rows = _expert_grid(rows_per_expert, n_e, d_f, bm, bn)
    w_spec = pl.BlockSpec((None, None, d, bn), lambda e, j, i: (layer, e, 0, j))
    return pl.pallas_call(
        _up_kernel,
        out_shape=jax.ShapeDtypeStruct((m, d_f), BF16),
        grid=grid,
        in_specs=[pl.BlockSpec((bm, d), lambda e, j, i: (rows(e, j, i), 0)), w_spec, w_spec],
        out_specs=pl.BlockSpec((bm, bn), lambda e, j, i: (rows(e, j, i), j)),
        scratch_shapes=[pltpu.VMEM((d, bn), BF16), pltpu.VMEM((d, bn), BF16)],
        compiler_params=_params(("parallel", "parallel", "arbitrary")),
        name="moe_up",
    )(xe, w_gate, w_up)


def _down_kernel(h_ref, wd_ref, gate_ref, o_ref, wd_bf):
    @pl.when(pl.program_id(2) == 0)
    def _():
        wd_bf[...] = wd_ref[...].astype(BF16)

    y = jnp.dot(h_ref[...], wd_bf[...], preferred_element_type=F32)
    o_ref[...] = (y * gate_ref[:, 0:1]).astype(o_ref.dtype)


def moe_down(h, w_down, gate, layer, rows_per_expert):
    m, d_f = h.shape
    _, n_e, _, d = w_down.shape
    bm = _pick(rows_per_expert, 1024, 2 * SUBLANES)
    bn = _pick(d, 1024, LANES)
    grid, rows = _expert_grid(rows_per_expert, n_e, d, bm, bn)
    return pl.pallas_call(
        _down_kernel,
        out_shape=jax.ShapeDtypeStruct((m, d), BF16),
        grid=grid,
        in_specs=[pl.BlockSpec((bm, d_f), lambda e, j, i: (rows(e, j, i), 0)),
                  pl.BlockSpec((None, None, d_f, bn), lambda e, j, i: (layer, e, 0, j)),
                  pl.BlockSpec((bm, LANES), lambda e, j, i: (rows(e, j, i), 0))],
        out_specs=pl.BlockSpec((bm, bn), lambda e, j, i: (rows(e, j, i), j)),
        scratch_shapes=[pltpu.VMEM((d_f, bn), BF16)],
        compiler_params=_params(("parallel", "parallel", "arbitrary")),
        name="moe_down",
    )(h, w_down, gate)


def _combine_kernel(idx_hbm, ye_ref, x_hbm, o_hbm, idx_smem, buf, sems, *, tm, n_steps):
    del x_hbm
    s = pl.program_id(0)
    dma = _RowDma(idx_hbm, idx_smem, buf, sems, tm)

    @pl.when(s == 0)
    def _():
        dma.fetch_idx(0, 0)
        dma.gather(0, o_hbm)

    @pl.when(s > 0)
    def _():
        dma.wait_scatter(1, o_hbm)

    dma.fetch_idx(2 * s + 1, 1)
    dma.gather(1, o_hbm)
    for slot in range(2):
        dma.wait_gather(slot, o_hbm)
        buf[slot] = buf[slot] + ye_ref[slot * tm:(slot + 1) * tm].astype(F32)
        dma.scatter(slot, o_hbm)
    dma.wait_scatter(0, o_hbm)

    @pl.when(s + 1 < n_steps)
    def _():
        dma.fetch_idx(2 * s + 2, 0)
        dma.gather(0, o_hbm)

    @pl.when(s == n_steps - 1)
    def _():
        dma.wait_scatter(1, o_hbm)


def moe_combine(x, ye, idx_t, tm):
    m, d = x.shape
    n_steps = idx_t.shape[0] // 2
    return pl.pallas_call(
        functools.partial(_combine_kernel, tm=tm, n_steps=n_steps),
        out_shape=jax.ShapeDtypeStruct((m, d), F32),
        grid=(n_steps,),
        in_specs=[pl.BlockSpec(memory_space=pl.ANY),
                  pl.BlockSpec((2 * tm, d), lambda s: (s, 0)),
                  pl.BlockSpec(memory_space=pl.ANY)],
        out_specs=pl.BlockSpec(memory_space=pl.ANY),
        scratch_shapes=_row_dma_scratch(tm, d, F32),
        input_output_aliases={2: 0},
        compiler_params=pltpu.CompilerParams(dimension_semantics=("arbitrary",),
                                             vmem_limit_bytes=VMEM_LIMIT_BYTES),
        name="moe_combine",
    )(idx_t, ye, x)


def expert_choice_moe(x, norm_g, w_router, w_gate, w_up, w_down, layer, n_groups):
    m = x.shape[0]
    e = w_router.shape[1]
    cap = max(1, CAPACITY_FACTOR * (m // n_groups) // e)
    hn_packed, aff = router(x, norm_g, jnp.transpose(w_router).astype(BF16))
    idx, gate = select_tokens(aff, n_groups, cap)
    tm = _pick(cap, 512, SUBLANES * SUBLANES)
    assert (e * n_groups * cap) % (2 * tm) == 0
    idx_t = idx.reshape(-1, SUBLANES, tm // SUBLANES)
    xe = moe_gather(hn_packed, idx_t, tm)
    h = moe_up(xe, w_gate, w_up, layer, n_groups * cap)
    ye = moe_down(h, w_down, gate, layer, n_groups * cap)
    return moe_combine(x, ye, idx_t, tm)


def kernel(x_prompt, x_sample, mix_norm, ffn_norm, final_norm, ab_w_in, ab_conv_a, ab_conv_b,
           ab_conv_b_bias, ab_gate_a_w, ab_gate_a_b, ab_gate_x_w, ab_gate_x_b, ab_lambda, ab_w_out,
           attn_w_qkv, attn_w_o, attn_sink, moe_w_router, moe_w_gate, moe_w_up, moe_w_down):
    n0, t0, d = x_prompt.shape
    n1, t1, _ = x_sample.shape
    assert n0 * t0 == n1 * t1, "expert capacity is per group: groups must hold equally many tokens"
    depth = mix_norm.shape[0]
    d_conv = ab_conv_a.shape[-1]
    d_rnn = ab_conv_b.shape[-1]
    n_q = attn_sink.shape[-1]
    hd = attn_w_o.shape[1] // n_q
    n_kv = (attn_w_qkv.shape[-1] // hd - n_q) // 2
    seqs_conv = _Seqs(n0, t0, n1, t1, _pick(math.gcd(t0, t1), 512, SUBLANES))
    seqs_attn = _Seqs(n0, t0, n1, t1, BLOCK)

    x = jnp.concatenate([x_prompt.reshape(n0 * t0, d), x_sample.reshape(n1 * t1, d)], axis=0)
    for layer in range(depth):
        j = layer // 2
        hn = rmsnorm(x, mix_norm[layer], BF16)
        if layer % 2 == 0:
            proj = matmul([hn], [ab_w_in[j].astype(BF16)])
            y_a = conv_a(proj, ab_conv_a[j], seqs_conv, d_conv)
            xc = conv_b(proj, ab_conv_b[j], ab_conv_b_bias[j], seqs_conv, 3 * d_conv + d_rnn, d_rnn)
            w_cat = jnp.concatenate([ab_gate_a_w[j, 0], ab_gate_x_w[j, 0],
                                     ab_gate_a_w[j, 1], ab_gate_x_w[j, 1]], axis=-1).astype(BF16)
            a, u = lru_gates(xc, w_cat, ab_gate_a_b[j], ab_gate_x_b[j], ab_lambda[j])
            hs = lru_scan(a, u, seqs_conv)
            y_b = gate_out(proj, hs, 3 * d_conv)
            w_out = ab_w_out[j].astype(BF16)
            x = matmul([y_a, y_b], [w_out[:d_conv], w_out[d_conv:]], res=x, bn=512)
        else:
            qkv = matmul([hn], [attn_w_qkv[j].astype(BF16)], out_dtype=BF16)
            att = window_attention(qkv, attn_sink[j], seqs_attn, n_q, n_kv, hd)
            x = matmul([att], [attn_w_o[j].astype(BF16)], res=x, bn=512)
        x = expert_choice_moe(x, ffn_norm[layer], moe_w_router[layer], moe_w_gate, moe_w_up,
                              moe_w_down, layer, n_groups=2)
    y_prompt = rmsnorm(x, final_norm, F32, row0=0, rows=n0 * t0)
    y_sample = rmsnorm(x, final_norm, F32, row0=n0 * t0, rows=n1 * t1)
    return (y_prompt.reshape(n0, t0, d), y_sample.reshape(n1, t1, d))
```
